```python
import math
import jax
import jax.numpy as jnp
from jax import lax
import numpy as np

D_MODEL = 2048
BATCH = 8
SEQ = 2048
DEPTH = 2
DEC_BATCH = 32
DEC_SEQ = 8
PAST_LEN = 8192
PAGE_SIZE = 128

N_A_LAYERS = DEPTH // 2
N_B_LAYERS = DEPTH - N_A_LAYERS
N_DENSE = (DEPTH + 1) // 2
N_MOE = DEPTH // 2

RWKV_HEAD = 64
RWKV_HEADS = D_MODEL // RWKV_HEAD
DECAY_LORA = 96
AAA_LORA = 96
GATE_LORA = 256
GN_EPS = 64e-5

ATTN_HEADS = 16
ATTN_HEAD_DIM = D_MODEL // ATTN_HEADS
MOBA_BLOCK = 256
MOBA_TOPK = 3
MOBA_QCHUNK = 32
REL_BUCKETS = 32
REL_MAX_DIST = 128

D_FF = 5632
N_EXPERTS = 8
MOE_TOPK = 2
D_FF_EXPERT = 5632

RMS_EPS = 1e-6
NEG_INF = -1e30

kernel_name = 'yoco_rwkv7_moba_decoder_step'


def rmsnorm(x, g):
    xf = x.astype(jnp.float32)
    y = xf * lax.rsqrt(jnp.mean(xf * xf, axis=-1, keepdims=True) + RMS_EPS)
    return (y * g.astype(jnp.float32)).astype(x.dtype)


def swiglu(h, w_gate, w_up, w_down):
    return (jax.nn.silu(h @ w_gate) * (h @ w_up)) @ w_down


def moe_swiglu(h, w_router, b_router, w_gate, w_up, w_down):
    logits = (h @ w_router).astype(jnp.float32) + b_router.astype(jnp.float32)
    top_val, top_idx = lax.top_k(logits, MOE_TOPK)
    gates = jax.nn.softmax(top_val, axis=-1)
    out = jnp.zeros_like(h)
    for e in range(N_EXPERTS):
        g_e = jnp.sum(jnp.where(top_idx == e, gates, 0.0), axis=-1, keepdims=True).astype(h.dtype)
        out = out + g_e * swiglu(h, w_gate[e], w_up[e], w_down[e])
    return out


def rwkv7_time_mix(h, shift0, wkv0, mu, w_r, w_k, w_v, w_o, w0, w1, w2, a0, a1, a2,
                   g1, g2, k_k, k_a, r_k, gn_g, gn_b):
    B, T, D = h.shape
    H, N = RWKV_HEADS, RWKV_HEAD
    f32 = jnp.float32
    prev = jnp.concatenate([shift0[:, None, :].astype(h.dtype), h[:, :-1]], axis=1)
    xx = prev - h
    xr, xw, xk, xv, xa, xg = (h + xx * mu[j] for j in range(6))
    r = (xr @ w_r).astype(f32)
    k = (xk @ w_k).astype(f32)
    v = (xv @ w_v).astype(f32)
    log_w = -jax.nn.softplus(-(w0 + jnp.tanh(xw @ w1) @ w2).astype(f32)) - 0.5
    decay = jnp.exp(-jnp.exp(log_w))
    a = jax.nn.sigmoid((a0 + (xa @ a1) @ a2).astype(f32))
    g = jax.nn.sigmoid(xg @ g1) @ g2
    kk = (k * k_k.astype(f32)).reshape(B, T, H, N)
    kk = kk / jnp.maximum(jnp.linalg.norm(kk, axis=-1, keepdims=True), 1e-12)
    k = k * (1.0 + (a - 1.0) * k_a.astype(f32))

    def tm(t):
        return t.reshape(B, T, H, N).transpose(1, 0, 2, 3)

    def step(S, inp):
        r_t, w_t, k_t, v_t, kk_t, a_t = inp
        sa = jnp.einsum('bhvk,bhk->bhv', S, -kk_t)
        S = (S * w_t[:, :, None, :]
             + jnp.einsum('bhv,bhk->bhvk', sa, kk_t * a_t)
             + jnp.einsum('bhv,bhk->bhvk', v_t, k_t))
        return S, jnp.einsum('bhvk,bhk->bhv', S, r_t)

    S_fin, o = lax.scan(step, wkv0.astype(f32),
                        (tm(r), tm(decay), tm(k), tm(v), kk.transpose(1, 0, 2, 3), tm(a)))
    o = o.transpose(1, 0, 2, 3)
    mean = jnp.mean(o, axis=-1, keepdims=True)
    var = jnp.mean(jnp.square(o - mean), axis=-1, keepdims=True)
    o = ((o - mean) * lax.rsqrt(var + GN_EPS)).reshape(B, T, D) * gn_g.astype(f32) + gn_b.astype(f32)
    rh, kh, vh = r.reshape(B, T, H, N), k.reshape(B, T, H, N), v.reshape(B, T, H, N)
    bonus = jnp.sum(rh * kh * r_k.astype(f32), axis=-1, keepdims=True) * vh
    o = o + bonus.reshape(B, T, D)
    out = (o.astype(h.dtype) * g) @ w_o
    return out, S_fin.astype(wkv0.dtype), h[:, -1]


def t5_bucket(dist):
    exact = REL_BUCKETS // 2
    d = jnp.maximum(dist, 0)
    log_ratio = jnp.log(jnp.maximum(d, exact).astype(jnp.float32) / exact) / math.log(REL_MAX_DIST / exact)
    large = jnp.minimum(exact + (log_ratio * (REL_BUCKETS - exact)).astype(jnp.int32), REL_BUCKETS - 1)
    return jnp.where(d < exact, d, large)


def moba_sequence(q, q_pos, k, v, rel_bias):
    Tq, H, Dh = q.shape
    f32 = jnp.float32
    L = k.shape[0]
    pad = (-L) % MOBA_BLOCK
    nb = (L + pad) // MOBA_BLOCK
    kb = jnp.pad(k, ((0, pad), (0, 0), (0, 0))).reshape(nb, MOBA_BLOCK, H, Dh).transpose(2, 0, 1, 3)
    vb = jnp.pad(v, ((0, pad), (0, 0), (0, 0))).reshape(nb, MOBA_BLOCK, H, Dh).transpose(2, 0, 1, 3)
    kmean = jnp.mean(kb.astype(f32), axis=2)
    q_blk = q_pos // MOBA_BLOCK
    gate = jnp.einsum('thd,hnd->thn', q.astype(f32), kmean)
    fully_past = jnp.arange(nb)[None, None, :] < q_blk[:, None, None]
    gate = jnp.where(fully_past, gate, NEG_INF)
    topk = min(MOBA_TOPK, nb)
    _, top_idx = lax.top_k(gate, topk)
    top_ok = jnp.broadcast_to(jnp.arange(topk)[None, None, :] < q_blk[:, None, None], (Tq, H, topk))
    sel = jnp.concatenate([top_idx, jnp.broadcast_to(q_blk[:, None, None], (Tq, H, 1))], axis=-1)
    sel_ok = jnp.concatenate([top_ok, jnp.ones((Tq, H, 1), bool)], axis=-1)
    n_sel = topk + 1
    qc = math.gcd(Tq, MOBA_QCHUNK)
    nc = Tq // qc
    head3 = jnp.arange(H)[None, :, None]
    head4 = jnp.arange(H)[None, :, None, None]
    offs = jnp.arange(MOBA_BLOCK)
    scale = ATTN_HEAD_DIM ** -0.5

    def chunk(args):
        q_c, pos_c, sel_c, ok_c = args
        kg = kb[head3, sel_c].astype(f32)
        vg = vb[head3, sel_c].astype(f32)
        kpos = sel_c[..., None] * MOBA_BLOCK + offs
        dist = pos_c[:, None, None, None] - kpos
        mask = ok_c[..., None] & (dist >= 0)
        bias = rel_bias[t5_bucket(dist), head4].astype(f32)
        logits = jnp.einsum('thd,thskd->thsk', q_c.astype(f32), kg) * scale + bias
        logits = jnp.where(mask, logits, NEG_INF).reshape(qc, H, n_sel * MOBA_BLOCK)
        p = jax.nn.softmax(logits, axis=-1)
        o = jnp.einsum('thk,thkd->thd', p, vg.reshape(qc, H, n_sel * MOBA_BLOCK, Dh))
        return o.astype(q.dtype)

    out = lax.map(chunk, (q.reshape(nc, qc, H, Dh), q_pos.reshape(nc, qc),
                          sel.reshape(nc, qc, H, n_sel), sel_ok.reshape(nc, qc, H, n_sel)))
    return out.reshape(Tq, H, Dh)


def moba_prompt(q, k, v, rel_bias):
    pos = jnp.arange(q.shape[1])
    return lax.map(lambda a: moba_sequence(a[0], pos, a[1], a[2], rel_bias), (q, k, v))


def moba_sample(q, k_new, v_new, cache_k, cache_v, page_table, rel_bias):
    pos = PAST_LEN + jnp.arange(q.shape[1])

    def one(args):
        q_b, kn, vn, pt = args
        k_past = cache_k[pt].reshape(-1, ATTN_HEADS, ATTN_HEAD_DIM)
        v_past = cache_v[pt].reshape(-1, ATTN_HEADS, ATTN_HEAD_DIM)
        k_all = jnp.concatenate([k_past, kn.astype(k_past.dtype)], axis=0)
        v_all = jnp.concatenate([v_past, vn.astype(v_past.dtype)], axis=0)
        return moba_sequence(q_b, pos, k_all, v_all, rel_bias)

    return lax.map(one, (q, k_new, v_new, page_table))


def setup_inputs(seed: int = 0) -> dict:
    key = jax.random.key(seed)
    ks = list(jax.random.split(key, 48))
    f32 = jnp.float32
    D, H, N = D_MODEL, RWKV_HEADS, RWKV_HEAD

    def nrm(shape, scale):
        return scale * jax.random.normal(ks.pop(), shape, f32)

    def unif(shape, lo, hi):
        return jax.random.uniform(ks.pop(), shape, f32, lo, hi)

    n_pages = PAST_LEN // PAGE_SIZE
    n_used = DEC_BATCH * n_pages
    n_phys = n_used + max(1, n_used // 4)
    page_table = jax.random.permutation(ks.pop(), n_phys)[:n_used].reshape(DEC_BATCH, n_pages).astype(jnp.int32)
    return {
        'x_prompt': nrm((BATCH, SEQ, D), 1.0),
        'x_sample': nrm((DEC_BATCH, DEC_SEQ, D), 1.0),
        'state_wkv': nrm((N_A_LAYERS, DEC_BATCH, H, N, N), 1.0),
        'state_shift': nrm((N_A_LAYERS, DEC_BATCH, D), 1.0),
        'cache_k': nrm((n_phys, PAGE_SIZE, ATTN_HEADS, ATTN_HEAD_DIM), 1.0),
        'cache_v': nrm((n_phys, PAGE_SIZE, ATTN_HEADS, ATTN_HEAD_DIM), 1.0),
        'page_table': page_table,
        'norm_mix': 1.0 + nrm((DEPTH, D), 0.02),
        'norm_ffn': 1.0 + nrm((DEPTH, D), 0.02),
        'norm_kv': 1.0 + nrm((D,), 0.02),
        'norm_final': 1.0 + nrm((D,), 0.02),
        'rwkv_mu': unif((N_A_LAYERS, 6, D), 0.0, 1.0),
        'rwkv_w_r': nrm((N_A_LAYERS, D, D), D ** -0.5),
        'rwkv_w_k': nrm((N_A_LAYERS, D, D), D ** -0.5),
        'rwkv_w_v': nrm((N_A_LAYERS, D, D), D ** -0.5),
        'rwkv_w_o': nrm((N_A_LAYERS, D, D), D ** -0.5),
        'rwkv_w0': unif((N_A_LAYERS, D), -6.0, 0.5),
        'rwkv_w1': nrm((N_A_LAYERS, D, DECAY_LORA), D ** -0.5),
        'rwkv_w2': nrm((N_A_LAYERS, DECAY_LORA, D), 0.5 * DECAY_LORA ** -0.5),
        'rwkv_a0': nrm((N_A_LAYERS, D), 0.5),
        'rwkv_a1': nrm((N_A_LAYERS, D, AAA_LORA), D ** -0.5),
        'rwkv_a2': nrm((N_A_LAYERS, AAA_LORA, D), AAA_LORA ** -0.5),
        'rwkv_g1': nrm((N_A_LAYERS, D, GATE_LORA), D ** -0.5),
        'rwkv_g2': nrm((N_A_LAYERS, GATE_LORA, D), GATE_LORA ** -0.5),
        'rwkv_k_k': 0.85 + nrm((N_A_LAYERS, D), 0.05),
        'rwkv_k_a': 1.0 + nrm((N_A_LAYERS, D), 0.05),
        'rwkv_r_k': nrm((N_A_LAYERS, H, N), 0.1),
        'rwkv_gn_g': 1.0 + nrm((N_A_LAYERS, D), 0.02),
        'rwkv_gn_b': nrm((N_A_LAYERS, D), 0.02),
        'kv_w': nrm((D, 2 * D), D ** -0.5),
        'attn_w_q': nrm((N_B_LAYERS, D, D), D ** -0.5),
        'attn_w_o': nrm((N_B_LAYERS, D, D), D ** -0.5),
        'rel_bias': nrm((REL_BUCKETS, ATTN_HEADS), 0.5),
        'ffn_w_gate': nrm((N_DENSE, D, D_FF), D ** -0.5),
        'ffn_w_up': nrm((N_DENSE, D, D_FF), D ** -0.5),
        'ffn_w_down': nrm((N_DENSE, D_FF, D), D_FF ** -0.5),
        'moe_w_router': nrm((N_MOE, D, N_EXPERTS), D ** -0.5),
        'moe_b_router': nrm((N_MOE, N_EXPERTS), 0.01),
        'moe_w_gate': nrm((N_MOE, N_EXPERTS, D, D_FF_EXPERT), D ** -0.5),
        'moe_w_up': nrm((N_MOE, N_EXPERTS, D, D_FF_EXPERT), D ** -0.5),
        'moe_w_down': nrm((N_MOE, N_EXPERTS, D_FF_EXPERT, D), D_FF_EXPERT ** -0.5),
    }


def reference(x_prompt, x_sample, state_wkv, state_shift, cache_k, cache_v, page_table,
              norm_mix, norm_ffn, norm_kv, norm_final,
              rwkv_mu, rwkv_w_r, rwkv_w_k, rwkv_w_v, rwkv_w_o, rwkv_w0, rwkv_w1, rwkv_w2,
              rwkv_a0, rwkv_a1, rwkv_a2, rwkv_g1, rwkv_g2, rwkv_k_k, rwkv_k_a, rwkv_r_k,
              rwkv_gn_g, rwkv_gn_b,
              kv_w, attn_w_q, attn_w_o, rel_bias,
              ffn_w_gate, ffn_w_up, ffn_w_down,
              moe_w_router, moe_b_router, moe_w_gate, moe_w_up, moe_w_down):

    def run_group(x, wkv0, shift0, attend):
        B, T, _ = x.shape
        new_wkv, new_shift = [], []
        k_sh = v_sh = None
        for l in range(DEPTH):
            if l < N_A_LAYERS:
                i = l
                h = rmsnorm(x, norm_mix[l])
                y, s_w, s_s = rwkv7_time_mix(
                    h, shift0[i], wkv0[i], rwkv_mu[i], rwkv_w_r[i], rwkv_w_k[i], rwkv_w_v[i], rwkv_w_o[i],
                    rwkv_w0[i], rwkv_w1[i], rwkv_w2[i], rwkv_a0[i], rwkv_a1[i], rwkv_a2[i],
                    rwkv_g1[i], rwkv_g2[i], rwkv_k_k[i], rwkv_k_a[i], rwkv_r_k[i], rwkv_gn_g[i], rwkv_gn_b[i])
                new_wkv.append(s_w)
                new_shift.append(s_s)
                x = x + y
            else:
                i = l - N_A_LAYERS
                if k_sh is None:
                    kv = rmsnorm(x, norm_kv) @ kv_w
                    k_sh = kv[..., :D_MODEL].reshape(B, T, ATTN_HEADS, ATTN_HEAD_DIM)
                    v_sh = kv[..., D_MODEL:].reshape(B, T, ATTN_HEADS, ATTN_HEAD_DIM)
                h = rmsnorm(x, norm_mix[l])
                q = (h @ attn_w_q[i]).reshape(B, T, ATTN_HEADS, ATTN_HEAD_DIM)
                o = attend(q, k_sh, v_sh)
                x = x + o.reshape(B, T, D_MODEL) @ attn_w_o[i]
            h = rmsnorm(x, norm_ffn[l])
            if l % 2 == 0:
                j = l // 2
                x = x + swiglu(h, ffn_w_gate[j], ffn_w_up[j], ffn_w_down[j])
            else:
                j = l // 2
                x = x + moe_swiglu(h, moe_w_router[j], moe_b_router[j], moe_w_gate[j], moe_w_up[j], moe_w_down[j])
        return rmsnorm(x, norm_final), jnp.stack(new_wkv), jnp.stack(new_shift), k_sh, v_sh

    bp = x_prompt.shape[0]
    wkv_zero = jnp.zeros((N_A_LAYERS, bp, RWKV_HEADS, RWKV_HEAD, RWKV_HEAD), state_wkv.dtype)
    shift_zero = jnp.zeros((N_A_LAYERS, bp, D_MODEL), state_shift.dtype)
    y_prompt, wkv_prompt, shift_prompt, k_prompt, v_prompt = run_group(
        x_prompt, wkv_zero, shift_zero,
        lambda q, k, v: moba_prompt(q, k, v, rel_bias))
    y_sample, wkv_sample, shift_sample, k_sample, v_sample = run_group(
        x_sample, state_wkv, state_shift,
        lambda q, k, v: moba_sample(q, k, v, cache_k, cache_v, page_table, rel_bias))
    return (y_prompt, y_sample, wkv_prompt, shift_prompt, k_prompt, v_prompt,
            wkv_sample, shift_sample, k_sample, v_sample)
```

```python
import functools
import math

import jax
import jax.numpy as jnp
from jax import lax
from jax.experimental import pallas as pl
from jax.experimental.pallas import tpu as pltpu

F32 = jnp.float32
BF16 = jnp.bfloat16

RWKV_HEAD = 64
ATTN_HEAD_DIM = 128
MOBA_BLOCK = 256
MOBA_TOPK = 3
REL_BUCKETS = 32
REL_MAX_DIST = 128
MOE_TOPK = 2
GN_EPS = 64e-5
RMS_EPS = 1e-6
NEG_INF = -1e30

V7X_VMEM_LIMIT = 48 * 1024 * 1024
LANES = 128

_HI = lax.Precision.HIGHEST


def _params(*sem):
    return pltpu.CompilerParams(dimension_semantics=sem, vmem_limit_bytes=V7X_VMEM_LIMIT)


def _dot(a, b):
    return jnp.dot(a, b, preferred_element_type=F32)


def _dot_hi(a, b):
    return jnp.dot(a, b, preferred_element_type=F32, precision=_HI)


def _dot_nt(a, b, precision=None):
    return lax.dot_general(a, b, (((1,), (1,)), ((), ())), preferred_element_type=F32, precision=precision)


def _pick(n, prefs):
    for p in prefs:
        if n % p == 0:
            return p
    return n


def _mm_kernel(x_ref, w_ref, *rest, has_res):
    if has_res:
        res_ref, o_ref = rest
    else:
        (o_ref,) = rest
    acc = _dot(x_ref[...], w_ref[...])
    if has_res:
        acc = acc + res_ref[...]
    o_ref[...] = acc.astype(o_ref.dtype)


def matmul(x, w, res=None, out_dtype=F32):
    m, k = x.shape
    n = w.shape[1]
    tm = _pick(m, (1024, 512, 256, 128, 64, 32, 16, 8))
    tn = _pick(n, (512, 256, 128))
    in_specs = [pl.BlockSpec((tm, k), lambda i, j: (i, 0)), pl.BlockSpec((k, tn), lambda i, j: (0, j))]
    args = [x, w]
    if res is not None:
        in_specs.append(pl.BlockSpec((tm, tn), lambda i, j: (i, j)))
        args.append(res)
    return pl.pallas_call(
        functools.partial(_mm_kernel, has_res=res is not None),
        grid=(m // tm, n // tn),
        in_specs=in_specs,
        out_specs=pl.BlockSpec((tm, tn), lambda i, j: (i, j)),
        out_shape=jax.ShapeDtypeStruct((m, n), out_dtype),
        compiler_params=_params("parallel", "arbitrary"),
        name="matmul",
    )(*args)


def _rms(xf):
    return xf * lax.rsqrt(jnp.mean(xf * xf, axis=-1, keepdims=True) + RMS_EPS)


def _norm_kernel(x_ref, g_ref, *o_refs):
    y = _rms(x_ref[...])
    for j, o_ref in enumerate(o_refs):
        o_ref[...] = (y * g_ref[j:j + 1, :]).astype(o_ref.dtype)


def rmsnorm_multi(x, gains, out_dtypes):
    m, d = x.shape
    n = gains.shape[0]
    tm = _pick(m, (512, 256, 128, 64, 32, 16, 8))
    return pl.pallas_call(
        _norm_kernel,
        grid=(m // tm,),
        in_specs=[pl.BlockSpec((tm, d), lambda i: (i, 0)), pl.BlockSpec((n, d), lambda i: (0, 0))],
        out_specs=[pl.BlockSpec((tm, d), lambda i: (i, 0)) for _ in range(n)],
        out_shape=[jax.ShapeDtypeStruct((m, d), dt) for dt in out_dtypes],
        compiler_params=_params("parallel"),
        name="rmsnorm",
    )(x, gains)


def _mix_kernel(x_ref, g_ref, mu_ref, s0_ref, *rest, blocks_per_seq):
    o_refs, last_ref, carry_ref = rest[:6], rest[6], rest[7]
    i = pl.program_id(0)
    h = _rms(x_ref[...]) * g_ref[...]
    tm = h.shape[0]

    @pl.when(i % blocks_per_seq == 0)
    def _():
        carry_ref[...] = s0_ref[0]

    row = lax.broadcasted_iota(jnp.int32, h.shape, 0)
    prev = jnp.where(row == 0, carry_ref[...], pltpu.roll(h, 1, 0))
    xx = prev - h
    for j in range(6):
        o_refs[j][...] = (h + xx * mu_ref[j:j + 1, :]).astype(o_refs[j].dtype)
    last = h[tm - 1:tm, :]
    carry_ref[...] = last
    last_ref[0] = last


def rwkv_mix(x, gain, mu, shift0, seq_len):
    m, d = x.shape
    b = m // seq_len
    tm = _pick(seq_len, (256, 128, 64, 32, 16, 8))
    bps = seq_len // tm
    outs = pl.pallas_call(
        functools.partial(_mix_kernel, blocks_per_seq=bps),
        grid=(m // tm,),
        in_specs=[
            pl.BlockSpec((tm, d), lambda i: (i, 0)),
            pl.BlockSpec((1, d), lambda i: (0, 0)),
            pl.BlockSpec((6, d), lambda i: (0, 0)),
            pl.BlockSpec((1, 1, d), lambda i: (i // bps, 0, 0)),
        ],
        out_specs=[pl.BlockSpec((tm, d), lambda i: (i, 0)) for _ in range(6)]
        + [pl.BlockSpec((1, 1, d), lambda i: (i // bps, 0, 0))],
        out_shape=[jax.ShapeDtypeStruct((m, d), BF16) for _ in range(6)]
        + [jax.ShapeDtypeStruct((b, 1, d), F32)],
        scratch_shapes=[pltpu.VMEM((1, d), F32)],
        compiler_params=_params("arbitrary"),
        name="rwkv_mix",
    )(x, gain.reshape(1, d), mu, shift0.reshape(b, 1, d))
    return outs[:6], outs[6].reshape(b, d)


def _lora_kernel(x_ref, w1_ref, w2_ref, o_ref, *, act):
    t = _dot(x_ref[...], w1_ref[...])
    if act == "tanh":
        t = jnp.tanh(t)
    elif act == "sigmoid":
        t = 1.0 / (1.0 + jnp.exp(-t))
    o_ref[...] = _dot(t.astype(BF16), w2_ref[...])


def lora(x, w1, w2, act):
    m, d = x.shape
    r = w1.shape[1]
    n = w2.shape[1]
    tm = _pick(m, (512, 256, 128, 64, 32, 16, 8))
    return pl.pallas_call(
        functools.partial(_lora_kernel, act=act),
        grid=(m // tm,),
        in_specs=[
            pl.BlockSpec((tm, d), lambda i: (i, 0)),
            pl.BlockSpec((d, r), lambda i: (0, 0)),
            pl.BlockSpec((r, n), lambda i: (0, 0)),
        ],
        out_specs=pl.BlockSpec((tm, n), lambda i: (i, 0)),
        out_shape=jax.ShapeDtypeStruct((m, n), F32),
        compiler_params=_params("parallel"),
        name="lora",
    )(x, w1, w2)


def _pad_rank(w1, w2):
    r = w1.shape[1]
    rp = -(-r // LANES) * LANES
    if rp != r:
        w1 = jnp.pad(w1, ((0, 0), (0, rp - r)))
        w2 = jnp.pad(w2, ((0, rp - r), (0, 0)))
    return w1.astype(BF16), w2.astype(BF16)


def _wkv_kernel(r_ref, k_ref, v_ref, wl_ref, al_ref, g_ref, prm_ref, h0_ref, y_ref, hout_ref, h_scr,
                *, chunk, heads):
    c_idx = pl.program_id(2)
    n = RWKV_HEAD
    tb = r_ref.shape[0]
    n_chunks = tb // chunk

    @pl.when(c_idx == 0)
    def _():
        h_scr[...] = h0_ref[0]

    ri = lax.broadcasted_iota(jnp.int32, (chunk, chunk), 0)
    ci = lax.broadcasted_iota(jnp.int32, (chunk, chunk), 1)
    tri_incl = (ri >= ci).astype(F32)
    strict = ri > ci
    eye_c = (ri == ci).astype(F32)
    rn = lax.broadcasted_iota(jnp.int32, (n, n), 0)
    cn = lax.broadcasted_iota(jnp.int32, (n, n), 1)
    eye_n = (rn == cn).astype(F32)
    n_double = max(int(math.log2(chunk)) - 1, 0)

    def one_chunk(ci_, carry):
        rows = pl.ds(pl.multiple_of(ci_ * chunk, chunk), chunk)
        r_all = r_ref[rows, :]
        k_all = k_ref[rows, :]
        v_all = v_ref[rows, :]
        wl_all = wl_ref[rows, :]
        al_all = al_ref[rows, :]
        g_all = g_ref[rows, :]
        outs = []
        for j in range(heads):
            sl = slice(j * n, (j + 1) * n)
            r, k, v = r_all[:, sl], k_all[:, sl], v_all[:, sl]
            w0, a0 = prm_ref[0:1, sl], prm_ref[1:2, sl]
            k_k, k_a = prm_ref[2:3, sl], prm_ref[3:4, sl]
            r_k, gn_g, gn_b = prm_ref[4:5, sl], prm_ref[5:6, sl], prm_ref[6:7, sl]
            z = -(w0 + wl_all[:, sl])
            softplus = jnp.maximum(z, 0.0) + jnp.log(1.0 + jnp.exp(-jnp.abs(z)))
            ld = -jnp.exp(-softplus - 0.5)
            alpha = 1.0 / (1.0 + jnp.exp(-(a0 + al_all[:, sl])))
            kkv = k * k_k
            kk = kkv / jnp.maximum(jnp.sqrt(jnp.sum(kkv * kkv, axis=-1, keepdims=True)), 1e-12)
            k2 = k * (1.0 + (alpha - 1.0) * k_a)

            cum = _dot_hi(tri_incl, ld)
            w_in = jnp.exp(cum)
            w_ex = jnp.exp(cum - ld)
            w_inv = jnp.exp(-cum)
            w_tot = w_in[chunk - 1:chunk, :]
            ar = jnp.concatenate([-kk * w_ex, r * w_in], axis=0)
            bk = jnp.concatenate([kk * alpha * w_inv, k2 * w_inv], axis=0)
            big = _dot_nt(ar, bk, _HI)
            a_ab = jnp.where(strict, big[:chunk, :chunk], 0.0)
            a_ak = jnp.where(strict, big[:chunk, chunk:], 0.0)
            a_r = jnp.concatenate([jnp.where(ri >= ci, big[chunk:, :chunk], 0.0),
                                   jnp.where(ri >= ci, big[chunk:, chunk:], 0.0)], axis=1)
            tinv = eye_c + a_ab
            p = a_ab
            for _ in range(n_double):
                p = _dot_hi(p, p)
                tinv = tinv + _dot_hi(tinv, p)

            h = h_scr[j]
            hdep = _dot_hi(ar, h)
            u = _dot_hi(tinv, hdep[:chunk] + _dot_hi(a_ak, v))
            uv = jnp.concatenate([u, v], axis=0)
            o = hdep[chunk:] + _dot_hi(a_r, uv)
            bk_t = _dot_nt(eye_n, bk * w_tot, _HI)
            upd = jnp.concatenate([eye_n * w_tot, bk_t], axis=1)
            h_scr[j] = _dot_hi(upd, jnp.concatenate([h, uv], axis=0))

            mean = jnp.mean(o, axis=-1, keepdims=True)
            var = jnp.mean(jnp.square(o - mean), axis=-1, keepdims=True)
            on = (o - mean) * lax.rsqrt(var + GN_EPS) * gn_g + gn_b
            bonus = jnp.sum(r * k2 * r_k, axis=-1, keepdims=True) * v
            outs.append(on + bonus)
        y_ref[rows, :] = (jnp.concatenate(outs, axis=1) * g_all).astype(y_ref.dtype)
        return carry

    lax.fori_loop(0, n_chunks, one_chunk, 0)

    @pl.when(c_idx == pl.num_programs(2) - 1)
    def _():
        hout_ref[0] = h_scr[...]


def wkv(r, k, v, wl, al, g, prm, h0, seq_len):
    m, d = r.shape
    b = m // seq_len
    n = RWKV_HEAD
    nh = d // n
    heads = 4
    chunk = min(64, seq_len)
    tb = _pick(seq_len, (512, 256, 128, 64, 32, 16, 8))
    nb = seq_len // tb
    wd = heads * n
    row_spec = pl.BlockSpec((tb, wd), lambda bi, hi, ci: (bi * nb + ci, hi))
    st_spec = pl.BlockSpec((1, heads, n, n), lambda bi, hi, ci: (bi, hi, 0, 0))
    return pl.pallas_call(
        functools.partial(_wkv_kernel, chunk=chunk, heads=heads),
        grid=(b, nh // heads, nb),
        in_specs=[row_spec] * 6 + [pl.BlockSpec((8, wd), lambda bi, hi, ci: (0, hi)), st_spec],
        out_specs=[row_spec, st_spec],
        out_shape=[jax.ShapeDtypeStruct((m, d), BF16), jax.ShapeDtypeStruct((b, nh, n, n), F32)],
        scratch_shapes=[pltpu.VMEM((heads, n, n), F32)],
        compiler_params=_params("parallel", "parallel", "arbitrary"),
        name="wkv",
    )(r, k, v, wl, al, g, prm, h0)


def _silu_mul(g, u):
    return g * (1.0 / (1.0 + jnp.exp(-g))) * u


def _ffn_kernel(h_ref, wg_ref, wu_ref, wd_ref, res_ref, o_ref, acc_ref):
    j = pl.program_id(1)

    @pl.when(j == 0)
    def _():
        acc_ref[...] = res_ref[...]

    h = h_ref[...]
    a = _silu_mul(_dot(h, wg_ref[...]), _dot(h, wu_ref[...])).astype(BF16)
    acc_ref[...] += _dot(a, wd_ref[...])

    @pl.when(j == pl.num_programs(1) - 1)
    def _():
        o_ref[...] = acc_ref[...]


def ffn(h, wg, wu, wd, res):
    m, d = h.shape
    f = wg.shape[1]
    tm = _pick(m, (512, 256, 128, 64, 32, 16, 8))
    tf = _pick(f, (512, 256, 128))
    return pl.pallas_call(
        _ffn_kernel,
        grid=(m // tm, f // tf),
        in_specs=[
            pl.BlockSpec((tm, d), lambda i, j: (i, 0)),
            pl.BlockSpec((d, tf), lambda i, j: (0, j)),
            pl.BlockSpec((d, tf), lambda i, j: (0, j)),
            pl.BlockSpec((tf, d), lambda i, j: (j, 0)),
            pl.BlockSpec((tm, d), lambda i, j: (i, 0)),
        ],
        out_specs=pl.BlockSpec((tm, d), lambda i, j: (i, 0)),
        out_shape=jax.ShapeDtypeStruct((m, d), F32),
        scratch_shapes=[pltpu.VMEM((tm, d), F32)],
        compiler_params=_params("parallel", "arbitrary"),
        name="ffn",
    )(h, wg, wu, wd, res)


def _moe_ffn_kernel(te_ref, tv_ref, x_ref, wg_ref, wu_ref, wd_ref, o_ref, acc_ref):
    i = pl.program_id(0)
    j = pl.program_id(1)

    @pl.when(tv_ref[i] > 0)
    def _():
        @pl.when(j == 0)
        def _():
            acc_ref[...] = jnp.zeros_like(acc_ref)

        x = x_ref[...].astype(BF16)
        a = _silu_mul(_dot(x, wg_ref[0]), _dot(x, wu_ref[0])).astype(BF16)
        acc_ref[...] += _dot(a, wd_ref[0])

        @pl.when(j == pl.num_programs(1) - 1)
        def _():
            o_ref[...] = acc_ref[...]

    @pl.when(tv_ref[i] == 0)
    def _():
        o_ref[...] = jnp.zeros_like(o_ref)


def moe_ffn(x_sorted, wg, wu, wd, tile_expert, tile_valid, tm):
    mp, d = x_sorted.shape
    f = wg.shape[2]
    tf = _pick(f, (512, 256, 128))
    nf = f // tf

    def fj(i, j, tv):
        return jnp.where(tv[i] > 0, j, nf - 1)

    grid_spec = pltpu.PrefetchScalarGridSpec(
        num_scalar_prefetch=2,
        grid=(mp // tm, nf),
        in_specs=[
            pl.BlockSpec((tm, d), lambda i, j, te, tv: (i, 0)),
            pl.BlockSpec((1, d, tf), lambda i, j, te, tv: (te[i], 0, fj(i, j, tv))),
            pl.BlockSpec((1, d, tf), lambda i, j, te, tv: (te[i], 0, fj(i, j, tv))),
            pl.BlockSpec((1, tf, d), lambda i, j, te, tv: (te[i], fj(i, j, tv), 0)),
        ],
        out_specs=pl.BlockSpec((tm, d), lambda i, j, te, tv: (i, 0)),
        scratch_shapes=[pltpu.VMEM((tm, d), F32)],
    )
    return pl.pallas_call(
        _moe_ffn_kernel,
        grid_spec=grid_spec,
        out_shape=jax.ShapeDtypeStruct((mp, d), F32),
        compiler_params=_params("parallel", "arbitrary"),
        name="moe_ffn",
    )(tile_expert, tile_valid, x_sorted, wg, wu, wd)


def _router_kernel(x_ref, g_ref, wr_ref, br_ref, h_ref, rt_ref):
    h = _rms(x_ref[...]) * g_ref[...]
    h_ref[...] = h
    logits = _dot_hi(h, wr_ref[...]) + br_ref[...]
    lane = lax.broadcasted_iota(jnp.int32, logits.shape, 1)
    m1 = jnp.max(logits, axis=-1, keepdims=True)
    i1 = jnp.min(jnp.where(logits == m1, lane, LANES), axis=-1, keepdims=True)
    rest = jnp.where(lane == i1, -jnp.inf, logits)
    m2 = jnp.max(rest, axis=-1, keepdims=True)
    i2 = jnp.min(jnp.where(rest == m2, lane, LANES), axis=-1, keepdims=True)
    e = jnp.exp(m2 - m1)
    g1 = 1.0 / (1.0 + e)
    g2 = e / (1.0 + e)
    rt_ref[...] = jnp.where(lane == 0, g1, jnp.where(lane == 1, g2, jnp.where(
        lane == 2, i1.astype(F32), jnp.where(lane == 3, i2.astype(F32), 0.0))))


def router(x, gain, w_router, b_router):
    m, d = x.shape
    e = w_router.shape[1]
    wr = jnp.pad(w_router.astype(F32), ((0, 0), (0, LANES - e)))
    br = jnp.pad(b_router.astype(F32), (0, LANES - e), constant_values=NEG_INF).reshape(1, LANES)
    tm = _pick(m, (512, 256, 128, 64, 32, 16, 8))
    return pl.pallas_call(
        _router_kernel,
        grid=(m // tm,),
        in_specs=[
            pl.BlockSpec((tm, d), lambda i: (i, 0)),
            pl.BlockSpec((1, d), lambda i: (0, 0)),
            pl.BlockSpec((d, LANES), lambda i: (0, 0)),
            pl.BlockSpec((1, LANES), lambda i: (0, 0)),
        ],
        out_specs=[pl.BlockSpec((tm, d), lambda i: (i, 0)), pl.BlockSpec((tm, LANES), lambda i: (i, 0))],
        out_shape=[jax.ShapeDtypeStruct((m, d), F32), jax.ShapeDtypeStruct((m, LANES), F32)],
        compiler_params=_params("parallel"),
        name="router",
    )(x, gain.reshape(1, d), wr, br)


def _row_copy(src_hbm, dst_ref, sem, src_row, dst_row):
    return pltpu.make_async_copy(src_hbm.at[pl.ds(src_row, 1)], dst_ref.at[pl.ds(dst_row, 1)], sem)


def _gather_rows(idx_ref, base, src_hbm, dst_ref, sem, n_rows):
    def start(r, c):
        _row_copy(src_hbm, dst_ref, sem, idx_ref[base + r], r).start()
        return c

    def wait(r, c):
        _row_copy(src_hbm, dst_ref, sem, 0, r).wait()
        return c

    lax.fori_loop(0, n_rows, start, 0)
    lax.fori_loop(0, n_rows, wait, 0)


def _gather_kernel(idx_ref, src_hbm, o_ref, sem):
    tr = o_ref.shape[0]
    _gather_rows(idx_ref, pl.program_id(0) * tr, src_hbm, o_ref, sem, tr)


def gather_rows(src, idx, tr):
    n = idx.shape[0]
    d = src.shape[1]
    grid_spec = pltpu.PrefetchScalarGridSpec(
        num_scalar_prefetch=1,
        grid=(n // tr,),
        in_specs=[pl.BlockSpec(memory_space=pl.ANY)],
        out_specs=pl.BlockSpec((tr, d), lambda i, idx: (i, 0)),
        scratch_shapes=[pltpu.SemaphoreType.DMA(())],
    )
    return pl.pallas_call(
        _gather_kernel,
        grid_spec=grid_spec,
        out_shape=jax.ShapeDtypeStruct((n, d), src.dtype),
        compiler_params=_params("arbitrary"),
        name="gather_rows",
    )(idx, src)


def _combine_kernel(p1_ref, p2_ref, y_hbm, x_ref, rt_ref, g_ref, o_ref, b1_ref, b2_ref, sem1, sem2):
    tr = x_ref.shape[0]
    base = pl.program_id(0) * tr

    def start(r, c):
        _row_copy(y_hbm, b1_ref, sem1, p1_ref[base + r], r).start()
        _row_copy(y_hbm, b2_ref, sem2, p2_ref[base + r], r).start()
        return c

    def wait(r, c):
        _row_copy(y_hbm, b1_ref, sem1, 0, r).wait()
        _row_copy(y_hbm, b2_ref, sem2, 0, r).wait()
        return c

    lax.fori_loop(0, tr, start, 0)
    lax.fori_loop(0, tr, wait, 0)
    rt = rt_ref[...]
    x = x_ref[...] + rt[:, 0:1] * b1_ref[...] + rt[:, 1:2] * b2_ref[...]
    o_ref[...] = _rms(x) * g_ref[...]


def moe_combine(x, y_sorted, rt, pos1, pos2, gain):
    m, d = x.shape
    tr = _pick(m, (128, 64, 32, 16, 8))
    grid_spec = pltpu.PrefetchScalarGridSpec(
        num_scalar_prefetch=2,
        grid=(m // tr,),
        in_specs=[
            pl.BlockSpec(memory_space=pl.ANY),
            pl.BlockSpec((tr, d), lambda i, p1, p2: (i, 0)),
            pl.BlockSpec((tr, LANES), lambda i, p1, p2: (i, 0)),
            pl.BlockSpec((1, d), lambda i, p1, p2: (0, 0)),
        ],
        out_specs=pl.BlockSpec((tr, d), lambda i, p1, p2: (i, 0)),
        scratch_shapes=[pltpu.VMEM((tr, d), F32), pltpu.VMEM((tr, d), F32),
                        pltpu.SemaphoreType.DMA(()), pltpu.SemaphoreType.DMA(())],
    )
    return pl.pallas_call(
        _combine_kernel,
        grid_spec=grid_spec,
        out_shape=jax.ShapeDtypeStruct((m, d), F32),
        compiler_params=_params("arbitrary"),
        name="moe_combine",
    )(pos1, pos2, y_sorted, x, rt, gain.reshape(1, d))


def moe_layer(x, gain_ffn, gain_final, w_router, b_router, wg, wu, wd):
    m, d = x.shape
    n_exp = wg.shape[0]
    h, rt = router(x, gain_ffn, w_router, b_router)
    tm = 512 if m * MOE_TOPK >= 8 * 512 else 64
    experts = rt[:, 2:4].astype(jnp.int32).reshape(-1)
    n_asg = experts.shape[0]
    counts = jnp.sum(experts[:, None] == jnp.arange(n_exp)[None, :], axis=0)
    padded = ((counts + tm - 1) // tm) * tm
    start = jnp.cumsum(counts) - counts
    pstart = jnp.cumsum(padded) - padded
    order = jnp.argsort(experts, stable=True)
    rank = jnp.zeros((n_asg,), jnp.int32).at[order].set(jnp.arange(n_asg, dtype=jnp.int32))
    dest = (pstart[experts] + rank - start[experts]).astype(jnp.int32)
    mp = n_asg + n_exp * tm
    src_token = jnp.zeros((mp,), jnp.int32).at[dest].set(jnp.arange(n_asg, dtype=jnp.int32) // MOE_TOPK)
    tile_start = jnp.arange(mp // tm, dtype=jnp.int32) * tm
    pend = jnp.cumsum(padded)
    tile_expert = jnp.minimum(jnp.sum(tile_start[:, None] >= pend[None, :], axis=1), n_exp - 1).astype(jnp.int32)
    tile_valid = (tile_start < pend[-1]).astype(jnp.int32)
    last_used = jnp.max(jnp.where(tile_valid > 0, tile_expert, 0))
    tile_expert = jnp.where(tile_valid > 0, tile_expert, last_used)

    x_sorted = gather_rows(h, src_token, tr=_pick(mp, (128, 64)))
    y_sorted = moe_ffn(x_sorted, wg, wu, wd, tile_expert, tile_valid, tm)
    pos = dest.reshape(m, MOE_TOPK)
    return moe_combine(x, y_sorted, rt, pos[:, 0], pos[:, 1], gain_final)


_T5_EXACT = REL_BUCKETS // 2
_T5_STEPS = [_T5_EXACT] + [
    math.ceil(_T5_EXACT * (REL_MAX_DIST / _T5_EXACT) ** (n / (REL_BUCKETS - _T5_EXACT)))
    for n in range(1, REL_BUCKETS - _T5_EXACT)]
T5_SATURATION = _T5_STEPS[-1]


def _rel_bias_of_dist(d, table):
    val = jnp.where(d >= 1, table(1), table(0))
    for b in range(2, _T5_EXACT):
        val = jnp.where(d >= b, table(b), val)
    for n, t in enumerate(_T5_STEPS):
        val = jnp.where(d >= t, table(_T5_EXACT + n), val)
    return val


def _prompt_bias_kernel(rb_ref, dg_ref, aj_ref):
    h = pl.program_id(0)
    blk = dg_ref.shape[1]
    ri = lax.broadcasted_iota(jnp.int32, (blk, blk), 0)
    ci = lax.broadcasted_iota(jnp.int32, (blk, blk), 1)
    table = lambda b: rb_ref[h, b]
    dg_ref[0] = _rel_bias_of_dist(ri - ci, table)
    aj_ref[0] = _rel_bias_of_dist(ri - ci + blk, table)


def prompt_bias(rel_bias):
    nh = rel_bias.shape[1]
    blk = MOBA_BLOCK
    tile = pl.BlockSpec((1, blk, blk), lambda h: (h, 0, 0))
    return pl.pallas_call(
        _prompt_bias_kernel,
        grid=(nh,),
        in_specs=[pl.BlockSpec(memory_space=pltpu.SMEM)],
        out_specs=[tile, tile],
        out_shape=[jax.ShapeDtypeStruct((nh, blk, blk), F32)] * 2,
        compiler_params=_params("arbitrary"),
        name="prompt_bias",
    )(rel_bias.T)


def _moba_prompt_kernel(c_last_ref, q_ref, k_ref, v_ref, dg_ref, aj_ref, o_ref, s_ref):
    h = pl.program_id(1)
    qi = pl.program_id(2)
    blk = MOBA_BLOCK
    scale = ATTN_HEAD_DIM ** -0.5
    q = q_ref[...]
    lane = lax.broadcasted_iota(jnp.int32, (blk, LANES), 1)
    c_last = c_last_ref[h]

    def rows(j):
        return pl.ds(pl.multiple_of(j * blk, blk), blk)

    def qk(j, gate):
        s = _dot_nt(q, k_ref[rows(j), :])
        s_ref[j] = s
        return jnp.where(lane == j, jnp.sum(s, axis=-1, keepdims=True), gate)

    gate = lax.fori_loop(0, qi, qk, jnp.full((blk, LANES), -jnp.inf, F32))
    cnt = jnp.zeros((blk, LANES), jnp.int32)
    for jp in range(k_ref.shape[0] // blk):
        col = gate[:, jp:jp + 1]
        cnt = cnt + jnp.where((col > gate) | ((col == gate) & (jp < lane)), 1, 0)
    sel = jnp.where((cnt < MOBA_TOPK) & (lane < qi), 1.0, 0.0)

    ri = lax.broadcasted_iota(jnp.int32, (blk, blk), 0)
    ci = lax.broadcasted_iota(jnp.int32, (blk, blk), 1)
    own = jnp.where(ri >= ci, _dot_nt(q, k_ref[rows(qi), :]) * scale + dg_ref[0], NEG_INF)

    def mask_max(j, m):
        bias = jnp.where(j == qi - 1, aj_ref[0], c_last)
        picked = jnp.sum(jnp.where(lane == j, sel, 0.0), axis=-1, keepdims=True)
        lm = jnp.where(picked > 0.0, s_ref[j] * scale + bias, NEG_INF)
        s_ref[j] = lm
        return jnp.maximum(m, jnp.max(lm, axis=-1, keepdims=True))

    m = lax.fori_loop(0, qi, mask_max, jnp.max(own, axis=-1, keepdims=True))

    def pv(j, carry):
        l, acc = carry
        p = jnp.exp(s_ref[j] - m)
        return l + jnp.sum(p, axis=-1, keepdims=True), acc + _dot(p.astype(BF16), v_ref[rows(j), :])

    p_own = jnp.exp(own - m)
    l, acc = lax.fori_loop(0, qi, pv, (jnp.sum(p_own, axis=-1, keepdims=True),
                                       _dot(p_own.astype(BF16), v_ref[rows(qi), :])))
    o_ref[...] = (acc / l).astype(o_ref.dtype)


def moba_prompt(q, k, v, rel_bias, seq_len):
    m, d = q.shape
    dh = ATTN_HEAD_DIM
    nh = d // dh
    b = m // seq_len
    blk = MOBA_BLOCK
    nq = seq_len // blk
    dg, aj = prompt_bias(rel_bias)
    kv_spec = pl.BlockSpec((seq_len, dh), lambda bi, hi, qi: (bi, hi))
    tile = pl.BlockSpec((1, blk, blk), lambda bi, hi, qi: (hi, 0, 0))
    q_spec = pl.BlockSpec((blk, dh), lambda bi, hi, qi: (bi * nq + qi, hi))
    return pl.pallas_call(
        _moba_prompt_kernel,
        grid=(b, nh, nq),
        in_specs=[pl.BlockSpec(memory_space=pltpu.SMEM), q_spec, kv_spec, kv_spec, tile, tile],
        out_specs=q_spec,
        out_shape=jax.ShapeDtypeStruct((m, d), BF16),
        scratch_shapes=[pltpu.VMEM((nq, blk, blk), F32)],
        compiler_params=_params("parallel", "parallel", "arbitrary"),
        name="moba_prompt",
    )(rel_bias[REL_BUCKETS - 1], q, k, v, dg, aj)


def _ksum_kernel(pt_ref, k_ref, o_ref, *, pages_per_block):
    p = pl.program_id(1)
    s = jnp.sum(k_ref[0], axis=0)

    @pl.when(p % pages_per_block == 0)
    def _():
        o_ref[0, 0] = s

    @pl.when(p % pages_per_block != 0)
    def _():
        o_ref[0, 0] += s


def cache_block_ksum(cache_k, page_table):
    b, n_pages = page_table.shape
    _, page, nh, dh = cache_k.shape
    ppb = MOBA_BLOCK // page
    grid_spec = pltpu.PrefetchScalarGridSpec(
        num_scalar_prefetch=1,
        grid=(b, n_pages),
        in_specs=[pl.BlockSpec((1, page, nh, dh), lambda bi, p, pt: (pt[bi, p], 0, 0, 0))],
        out_specs=pl.BlockSpec((1, 1, nh, dh), lambda bi, p, pt: (bi, p // ppb, 0, 0)),
    )
    return pl.pallas_call(
        functools.partial(_ksum_kernel, pages_per_block=ppb),
        grid_spec=grid_spec,
        out_shape=jax.ShapeDtypeStruct((b, n_pages // ppb, nh, dh), F32),
        compiler_params=_params("parallel", "arbitrary"),
        name="cache_block_ksum",
    )(page_table, cache_k)


def _sample_select_kernel(q_ref, ks_ref, sel_ref):
    nh = q_ref.shape[1]
    nblk = ks_ref.shape[2]
    gate = jnp.concatenate(
        [_dot_nt(ks_ref[0, h], q_ref[0, h].astype(F32), _HI) for h in range(nh)], axis=1)
    row = lax.broadcasted_iota(jnp.int32, gate.shape, 0)
    cnt = jnp.zeros(gate.shape, jnp.int32)
    for n in range(nblk):
        g_n = gate[n:n + 1, :]
        cnt = cnt + jnp.where((g_n > gate) | ((g_n == gate) & (n < row)), 1, 0)
    sel_ref[0] = jnp.where(cnt < MOBA_TOPK, 1.0, 0.0)


def sample_select(q_hq, ksum_h):
    b, nh, tq, dh = q_hq.shape
    nblk = ksum_h.shape[2]
    return pl.pallas_call(
        _sample_select_kernel,
        grid=(b,),
        in_specs=[pl.BlockSpec((1, nh, tq, dh), lambda i: (i, 0, 0, 0)),
                  pl.BlockSpec((1, nh, nblk, dh), lambda i: (i, 0, 0, 0))],
        out_specs=pl.BlockSpec((1, nblk, nh * tq), lambda i: (i, 0, 0)),
        out_shape=jax.ShapeDtypeStruct((b, nblk, nh * tq), F32),
        compiler_params=_params("parallel"),
        name="sample_select",
    )(q_hq, ksum_h)


def _sample_bias_kernel(rt_ref, last_ref, new_ref, *, n_heads, tq, page):
    rows = n_heads * tq
    table = lambda b: rt_ref[:, b:b + 1]
    ri = lax.broadcasted_iota(jnp.int32, last_ref.shape, 0)
    ci = lax.broadcasted_iota(jnp.int32, last_ref.shape, 1)
    last_ref[...] = _rel_bias_of_dist(page + ri % tq - ci // n_heads, table)
    ri = lax.broadcasted_iota(jnp.int32, (rows, rows), 0)
    ci = lax.broadcasted_iota(jnp.int32, (rows, rows), 1)
    d = ri % tq - ci // n_heads
    ok = (d >= 0) & (ci % n_heads == ri // tq)
    new_ref[...] = jnp.where(ok, _rel_bias_of_dist(d, table), NEG_INF)


def sample_bias(rel_bias, tq, page):
    nh = rel_bias.shape[1]
    rows = nh * tq
    rt = jnp.repeat(rel_bias.T, tq, axis=0)
    last, new = pl.pallas_call(
        functools.partial(_sample_bias_kernel, n_heads=nh, tq=tq, page=page),
        out_shape=[jax.ShapeDtypeStruct((rows, page * nh), F32), jax.ShapeDtypeStruct((rows, rows), F32)],
        compiler_params=pltpu.CompilerParams(vmem_limit_bytes=V7X_VMEM_LIMIT),
        name="sample_bias",
    )(rt)
    return rt, last, new


def _moba_sample_kernel(pt_ref, q_ref, kn_ref, vn_ref, k_ref, v_ref, sel_ref, rt_ref, last_ref, new_ref,
                        o_ref, m_ref, l_ref, acc_ref, *, n_heads, tq):
    p = pl.program_id(1)
    n_pages = pl.num_programs(1)
    scale = ATTN_HEAD_DIM ** -0.5
    q = q_ref[0]
    rows = q.shape[0]

    @pl.when(p == 0)
    def _():
        lm = _dot_nt(q, kn_ref[0]) * scale + new_ref[...]
        m = jnp.max(lm, axis=-1, keepdims=True)
        pr = jnp.exp(lm - m)
        m_ref[...] = m
        l_ref[...] = jnp.sum(pr, axis=-1, keepdims=True)
        acc_ref[...] = _dot(pr.astype(BF16), vn_ref[0])

    page, _, dh = k_ref.shape[1:]
    cols = page * n_heads
    kp = k_ref[0].reshape(cols, dh).astype(BF16)
    vp = v_ref[0].reshape(cols, dh).astype(BF16)
    s = _dot_nt(q, kp) * scale
    bias = jnp.where(p == n_pages - 1, last_ref[...], rt_ref[:, REL_BUCKETS - 1:REL_BUCKETS])
    ri = lax.broadcasted_iota(jnp.int32, (rows, rows), 0)
    ci = lax.broadcasted_iota(jnp.int32, (rows, rows), 1)
    picked = jnp.sum(jnp.where(ri == ci, sel_ref[0, 0], 0.0), axis=-1, keepdims=True)
    rh = lax.broadcasted_iota(jnp.int32, (rows, cols), 0) // tq
    ch = lax.broadcasted_iota(jnp.int32, (rows, cols), 1) % n_heads
    lm = jnp.where((rh == ch) & (picked > 0.0), s + bias, NEG_INF)
    m_old = m_ref[...]
    m_new = jnp.maximum(m_old, jnp.max(lm, axis=-1, keepdims=True))
    alpha = jnp.exp(m_old - m_new)
    pr = jnp.exp(lm - m_new)
    m_ref[...] = m_new
    l_ref[...] = alpha * l_ref[...] + jnp.sum(pr, axis=-1, keepdims=True)
    acc_ref[...] = alpha * acc_ref[...] + _dot(pr.astype(BF16), vp)

    @pl.when(p == n_pages - 1)
    def _():
        o_ref[0] = acc_ref[...] / l_ref[...]


def moba_sample(q, k_new, v_new, cache_k, cache_v, page_table, rel_bias, tq):
    m, d = q.shape
    b = m // tq
    n_pages = page_table.shape[1]
    _, page, nh, dh = cache_k.shape
    ppb = MOBA_BLOCK // page
    rows = nh * tq
    past = n_pages * page
    assert MOBA_BLOCK % page == 0 and past % MOBA_BLOCK == 0 and tq <= MOBA_BLOCK and page + 1 >= T5_SATURATION

    q_hq = q.reshape(b, tq, nh, dh).transpose(0, 2, 1, 3)
    ksum = cache_block_ksum(cache_k, page_table).transpose(0, 2, 1, 3)
    sel = sample_select(q_hq, ksum)
    sel = sel.reshape(b, past // MOBA_BLOCK, 1, rows)
    rt, bias_last, bias_new = sample_bias(rel_bias, tq, page)
    q_rows = q_hq.reshape(b, rows, dh)
    kn = k_new.astype(BF16).reshape(b, rows, dh)
    vn = v_new.astype(BF16).reshape(b, rows, dh)

    small = pl.BlockSpec((1, rows, dh), lambda bi, p, pt: (bi, 0, 0))
    page_spec = pl.BlockSpec((1, page, nh, dh), lambda bi, p, pt: (pt[bi, p], 0, 0, 0))
    const = lambda shape: pl.BlockSpec(shape, lambda bi, p, pt: (0,) * len(shape))
    grid_spec = pltpu.PrefetchScalarGridSpec(
        num_scalar_prefetch=1,
        grid=(b, n_pages),
        in_specs=[small, small, small, page_spec, page_spec,
                  pl.BlockSpec((1, 1, 1, rows), lambda bi, p, pt: (bi, p // ppb, 0, 0)),
                  const((rows, REL_BUCKETS)), const((rows, page * nh)), const((rows, rows))],
        out_specs=small,
        scratch_shapes=[pltpu.VMEM((rows, 1), F32), pltpu.VMEM((rows, 1), F32), pltpu.VMEM((rows, dh), F32)],
    )
    o = pl.pallas_call(
        functools.partial(_moba_sample_kernel, n_heads=nh, tq=tq),
        grid_spec=grid_spec,
        out_shape=jax.ShapeDtypeStruct((b, rows, dh), F32),
        compiler_params=_params("parallel", "arbitrary"),
        name="moba_sample",
    )(page_table, q_rows, kn, vn, cache_k, cache_v, sel, rt, bias_last, bias_new)
    return o.reshape(b, nh, tq, dh).transpose(0, 2, 1, 3).reshape(m, d).astype(BF16)


def rwkv_time_mix(x, seq_len, shift0, wkv0, gain, w):
    m, d = x.shape
    (xr, xw, xk, xv, xa, xg), shift_new = rwkv_mix(x, gain, w["mu"], shift0, seq_len)
    r = matmul(xr, w["w_r"])
    k = matmul(xk, w["w_k"])
    v = matmul(xv, w["w_v"])
    wl = lora(xw, w["w1"], w["w2"], "tanh")
    al = lora(xa, w["a1"], w["a2"], "none")
    g = lora(xg, w["g1"], w["g2"], "sigmoid")
    y, h_fin = wkv(r, k, v, wl, al, g, w["prm"], jnp.swapaxes(wkv0, -1, -2), seq_len)
    x_new = matmul(y, w["w_o"], res=x)
    return x_new, jnp.swapaxes(h_fin, -1, -2), shift_new


def _rwkv_weights(i, rwkv_mu, rwkv_w_r, rwkv_w_k, rwkv_w_v, rwkv_w_o, rwkv_w0, rwkv_w1, rwkv_w2, rwkv_a0, rwkv_a1,
                  rwkv_a2, rwkv_g1, rwkv_g2, rwkv_k_k, rwkv_k_a, rwkv_r_k, rwkv_gn_g, rwkv_gn_b):
    d = rwkv_mu.shape[-1]
    w1, w2 = _pad_rank(rwkv_w1[i], rwkv_w2[i])
    a1, a2 = _pad_rank(rwkv_a1[i], rwkv_a2[i])
    g1, g2 = _pad_rank(rwkv_g1[i], rwkv_g2[i])
    prm = jnp.stack([rwkv_w0[i], rwkv_a0[i], rwkv_k_k[i], rwkv_k_a[i], rwkv_r_k[i].reshape(d), rwkv_gn_g[i],
                     rwkv_gn_b[i], jnp.zeros((d,), F32)]).astype(F32)
    return dict(mu=rwkv_mu[i], w_r=rwkv_w_r[i].astype(BF16), w_k=rwkv_w_k[i].astype(BF16),
                w_v=rwkv_w_v[i].astype(BF16), w_o=rwkv_w_o[i].astype(BF16),
                w1=w1, w2=w2, a1=a1, a2=a2, g1=g1, g2=g2, prm=prm)


def kernel(x_prompt, x_sample, state_wkv, state_shift, cache_k, cache_v, page_table, norm_mix, norm_ffn, norm_kv, norm_final, rwkv_mu, rwkv_w_r, rwkv_w_k, rwkv_w_v, rwkv_w_o, rwkv_w0, rwkv_w1, rwkv_w2, rwkv_a0, rwkv_a1, rwkv_a2, rwkv_g1, rwkv_g2, rwkv_k_k, rwkv_k_a, rwkv_r_k, rwkv_gn_g, rwkv_gn_b, kv_w, attn_w_q, attn_w_o, rel_bias, ffn_w_gate, ffn_w_up, ffn_w_down, moe_w_router, moe_b_router, moe_w_gate, moe_w_up, moe_w_down):
    d = x_prompt.shape[-1]
    nh = d // ATTN_HEAD_DIM
    assert norm_mix.shape[0] == 2 and rwkv_mu.shape[0] == 1 and attn_w_q.shape[0] == 1
    assert ffn_w_gate.shape[0] == 1 and moe_w_gate.shape[0] == 1

    w_rwkv = _rwkv_weights(0, rwkv_mu, rwkv_w_r, rwkv_w_k, rwkv_w_v, rwkv_w_o, rwkv_w0, rwkv_w1, rwkv_w2, rwkv_a0,
                           rwkv_a1, rwkv_a2, rwkv_g1, rwkv_g2, rwkv_k_k, rwkv_k_a, rwkv_r_k, rwkv_gn_g, rwkv_gn_b)
    wg, wu, wd = ffn_w_gate[0].astype(BF16), ffn_w_up[0].astype(BF16), ffn_w_down[0].astype(BF16)
    kvw_k, kvw_v = kv_w[:, :d].astype(BF16), kv_w[:, d:].astype(BF16)
    wq, wo = attn_w_q[0].astype(BF16), attn_w_o[0].astype(BF16)
    eg, eu, ed = moe_w_gate[0].astype(BF16), moe_w_up[0].astype(BF16), moe_w_down[0].astype(BF16)
    gains_kvq = jnp.stack([norm_kv, norm_mix[1]])

    def run_group(x3, wkv0, shift0, attend):
        b, t, _ = x3.shape
        x = x3.reshape(b * t, d)
        x, wkv_new, shift_new = rwkv_time_mix(x, t, shift0, wkv0, norm_mix[0], w_rwkv)
        (h,) = rmsnorm_multi(x, norm_ffn[0:1], [BF16])
        x = ffn(h, wg, wu, wd, x)
        h_kv, h_q = rmsnorm_multi(x, gains_kvq, [BF16, BF16])
        k = matmul(h_kv, kvw_k)
        v = matmul(h_kv, kvw_v)
        q = matmul(h_q, wq, out_dtype=BF16)
        o = attend(q, k, v, t)
        x = matmul(o, wo, res=x)
        y = moe_layer(x, norm_ffn[1], norm_final, moe_w_router[0], moe_b_router[0], eg, eu, ed)
        kv_shape = (b, t, nh, ATTN_HEAD_DIM)
        return y.reshape(b, t, d), wkv_new[None], shift_new[None], k.reshape(kv_shape), v.reshape(kv_shape)

    bp = x_prompt.shape[0]
    wkv_zero = jnp.zeros((bp,) + state_wkv.shape[2:], state_wkv.dtype)
    shift_zero = jnp.zeros((bp, d), state_shift.dtype)
    out_p = run_group(
        x_prompt, wkv_zero, shift_zero,
        lambda q, k, v, t: moba_prompt(q, k.astype(BF16), v.astype(BF16), rel_bias, t))
    out_s = run_group(
        x_sample, state_wkv[0], state_shift[0],
        lambda q, k, v, t: moba_sample(q, k, v, cache_k, cache_v, page_table, rel_bias, t))
    return (out_p[0], out_s[0]) + out_p[1:] + out_s[1:]
```

```python
import functools
import math

import jax
import jax.numpy as jnp
from jax import lax
from jax.experimental import pallas as pl
from jax.experimental.pallas import tpu as pltpu

F32 = jnp.float32
BF16 = jnp.bfloat16

RWKV_HEAD = 64
ATTN_HEAD_DIM = 128
MOBA_BLOCK = 256
MOBA_TOPK = 3
REL_BUCKETS = 32
REL_MAX_DIST = 128
MOE_TOPK = 2
GN_EPS = 64e-5
RMS_EPS = 1e-6
NEG_INF = -1e30

V7X_VMEM_LIMIT = 48 * 1024 * 1024
LANES = 128

_HI = lax.Precision.HIGHEST


def _params(*sem):
    return pltpu.CompilerParams(dimension_semantics=sem, vmem_limit_bytes=V7X_VMEM_LIMIT)


def _dot(a, b):
    return jnp.dot(a, b, preferred_element_type=F32)


def _dot_hi(a, b):
    return jnp.dot(a, b, preferred_element_type=F32, precision=_HI)


def _dot_nt(a, b, precision=None):
    return lax.dot_general(a, b, (((1,), (1,)), ((), ())), preferred_element_type=F32, precision=precision)


def _pick(n, prefs):
    for p in prefs:
        if n % p == 0:
            return p
    return n


def _mm_kernel(x_ref, w_ref, *rest, has_res):
    if has_res:
        res_ref, o_ref = rest
    else:
        (o_ref,) = rest
    acc = _dot(x_ref[...], w_ref[...])
    if has_res:
        acc = acc + res_ref[...]
    o_ref[...] = acc.astype(o_ref.dtype)


def matmul(x, w, res=None, out_dtype=F32):
    m, k = x.shape
    n = w.shape[1]
    tm = _pick(m, (1024, 512, 256, 128, 64, 32, 16, 8))
    tn = _pick(n, (512, 256, 128))
    in_specs = [pl.BlockSpec((tm, k), lambda i, j: (i, 0)), pl.BlockSpec((k, tn), lambda i, j: (0, j))]
    args = [x, w]
    if res is not None:
        in_specs.append(pl.BlockSpec((tm, tn), lambda i, j: (i, j)))
        args.append(res)
    return pl.pallas_call(
        functools.partial(_mm_kernel, has_res=res is not None),
        grid=(m // tm, n // tn),
        in_specs=in_specs,
        out_specs=pl.BlockSpec((tm, tn), lambda i, j: (i, j)),
        out_shape=jax.ShapeDtypeStruct((m, n), out_dtype),
        compiler_params=_params("parallel", "arbitrary"),
        name="matmul",
    )(*args)


def _rms(xf):
    return xf * lax.rsqrt(jnp.mean(xf * xf, axis=-1, keepdims=True) + RMS_EPS)


def _norm_kernel(x_ref, g_ref, *o_refs):
    y = _rms(x_ref[...])
    for j, o_ref in enumerate(o_refs):
        o_ref[...] = (y * g_ref[j:j + 1, :]).astype(o_ref.dtype)


def rmsnorm_multi(x, gains, out_dtypes):
    m, d = x.shape
    n = gains.shape[0]
    tm = _pick(m, (512, 256, 128, 64, 32, 16, 8))
    return pl.pallas_call(
        _norm_kernel,
        grid=(m // tm,),
        in_specs=[pl.BlockSpec((tm, d), lambda i: (i, 0)), pl.BlockSpec((n, d), lambda i: (0, 0))],
        out_specs=[pl.BlockSpec((tm, d), lambda i: (i, 0)) for _ in range(n)],
        out_shape=[jax.ShapeDtypeStruct((m, d), dt) for dt in out_dtypes],
        compiler_params=_params("parallel"),
        name="rmsnorm",
    )(x, gains)


def _mix_kernel(x_ref, g_ref, mu_ref, s0_ref, *rest, blocks_per_seq):
    o_refs, last_ref, carry_ref = rest[:6], rest[6], rest[7]
    i = pl.program_id(0)
    h = _rms(x_ref[...]) * g_ref[...]
    tm = h.shape[0]

    @pl.when(i % blocks_per_seq == 0)
    def _():
        carry_ref[...] = s0_ref[0]

    row = lax.broadcasted_iota(jnp.int32, h.shape, 0)
    prev = jnp.where(row == 0, carry_ref[...], pltpu.roll(h, 1, 0))
    xx = prev - h
    for j in range(6):
        o_refs[j][...] = (h + xx * mu_ref[j:j + 1, :]).astype(o_refs[j].dtype)
    last = h[tm - 1:tm, :]
    carry_ref[...] = last
    last_ref[0] = last


def rwkv_mix(x, gain, mu, shift0, seq_len):
    m, d = x.shape
    b = m // seq_len
    tm = _pick(seq_len, (256, 128, 64, 32, 16, 8))
    bps = seq_len // tm
    outs = pl.pallas_call(
        functools.partial(_mix_kernel, blocks_per_seq=bps),
        grid=(m // tm,),
        in_specs=[
            pl.BlockSpec((tm, d), lambda i: (i, 0)),
            pl.BlockSpec((1, d), lambda i: (0, 0)),
            pl.BlockSpec((6, d), lambda i: (0, 0)),
            pl.BlockSpec((1, 1, d), lambda i: (i // bps, 0, 0)),
        ],
        out_specs=[pl.BlockSpec((tm, d), lambda i: (i, 0)) for _ in range(6)]
        + [pl.BlockSpec((1, 1, d), lambda i: (i // bps, 0, 0))],
        out_shape=[jax.ShapeDtypeStruct((m, d), BF16) for _ in range(6)]
        + [jax.ShapeDtypeStruct((b, 1, d), F32)],
        scratch_shapes=[pltpu.VMEM((1, d), F32)],
        compiler_params=_params("arbitrary"),
        name="rwkv_mix",
    )(x, gain.reshape(1, d), mu, shift0.reshape(b, 1, d))
    return outs[:6], outs[6].reshape(b, d)


def _lora_kernel(x_ref, w1_ref, w2_ref, o_ref, *, act):
    t = _dot(x_ref[...], w1_ref[...])
    if act == "tanh":
        t = jnp.tanh(t)
    elif act == "sigmoid":
        t = 1.0 / (1.0 + jnp.exp(-t))
    o_ref[...] = _dot(t.astype(BF16), w2_ref[...])


def lora(x, w1, w2, act):
    m, d = x.shape
    r = w1.shape[1]
    n = w2.shape[1]
    tm = _pick(m, (512, 256, 128, 64, 32, 16, 8))
    return pl.pallas_call(
        functools.partial(_lora_kernel, act=act),
        grid=(m // tm,),
        in_specs=[
            pl.BlockSpec((tm, d), lambda i: (i, 0)),
            pl.BlockSpec((d, r), lambda i: (0, 0)),
            pl.BlockSpec((r, n), lambda i: (0, 0)),
        ],
        out_specs=pl.BlockSpec((tm, n), lambda i: (i, 0)),
        out_shape=jax.ShapeDtypeStruct((m, n), F32),
        compiler_params=_params("parallel"),
        name="lora",
    )(x, w1, w2)


def _pad_rank(w1, w2):
    r = w1.shape[1]
    rp = -(-r // LANES) * LANES
    if rp != r:
        w1 = jnp.pad(w1, ((0, 0), (0, rp - r)))
        w2 = jnp.pad(w2, ((0, rp - r), (0, 0)))
    return w1.astype(BF16), w2.astype(BF16)


def _split(a):
    hi = a.astype(BF16)
    return hi, (a - hi.astype(F32)).astype(BF16)


def _dot3(a, b, mode):
    (ah, al), (bh, bl) = a, b
    la = 0 if mode == "tn" else 1
    lb = 1 if mode == "nt" else 0
    lhs = jnp.concatenate([ah, ah, al], axis=la)
    rhs = jnp.concatenate([bh, bl, bh], axis=lb)
    return lax.dot_general(lhs, rhs, (((la,), (lb,)), ((), ())), preferred_element_type=F32)


def _wkv_kernel(r_ref, k_ref, v_ref, wl_ref, al_ref, g_ref, prm_ref, h0_ref, y_ref, hout_ref, h_scr,
                *, chunk, heads):
    c_idx = pl.program_id(2)
    n = RWKV_HEAD
    tb = r_ref.shape[0]
    n_chunks = tb // chunk

    @pl.when(c_idx == 0)
    def _():
        h_scr[...] = h0_ref[0]

    ri = lax.broadcasted_iota(jnp.int32, (chunk, chunk), 0)
    ci = lax.broadcasted_iota(jnp.int32, (chunk, chunk), 1)
    strict = ri > ci
    incl = ri >= ci
    eye_c = jnp.where(ri == ci, 1.0, 0.0)
    tri = jnp.where(incl, 1.0, 0.0).astype(BF16)
    tri3 = jnp.concatenate([tri, tri, tri], axis=1)
    rn = lax.broadcasted_iota(jnp.int32, (n, n), 0)
    cn = lax.broadcasted_iota(jnp.int32, (n, n), 1)
    eye_n = jnp.where(rn == cn, 1.0, 0.0)
    n_double = max(int(math.log2(chunk)) - 1, 0)

    def one_chunk(ci_, hs):
        rows = pl.ds(pl.multiple_of(ci_ * chunk, chunk), chunk)
        r_all = r_ref[rows, :]
        k_all = k_ref[rows, :]
        v_all = v_ref[rows, :]
        wl_all = wl_ref[rows, :]
        al_all = al_ref[rows, :]
        hd = range(heads)
        sls = [slice(j * n, (j + 1) * n) for j in hd]
        prm = lambda row, j: prm_ref[row:row + 1, sls[j]]
        r = [r_all[:, sl] for sl in sls]
        k = [k_all[:, sl] for sl in sls]
        v = [v_all[:, sl] for sl in sls]
        z = [-(prm(0, j) + wl_all[:, sls[j]]) for j in hd]
        ld = [-jnp.exp(-(jnp.maximum(zj, 0.0) + jnp.log(1.0 + jnp.exp(-jnp.abs(zj)))) - 0.5) for zj in z]
        alpha = [1.0 / (1.0 + jnp.exp(-(prm(1, j) + al_all[:, sls[j]]))) for j in hd]
        kkv = [k[j] * prm(2, j) for j in hd]
        kk = [kkv[j] / jnp.maximum(jnp.sqrt(jnp.sum(kkv[j] * kkv[j], axis=-1, keepdims=True)), 1e-12) for j in hd]
        k2 = [k[j] * (1.0 + (alpha[j] - 1.0) * prm(3, j)) for j in hd]

        def ld3(x):
            x_h, x_m = _split(x)
            return jnp.concatenate([x_h, x_m, (x - x_h.astype(F32) - x_m.astype(F32)).astype(BF16)], axis=0)

        cum = [_dot(tri3, ld3(ld[j])) for j in hd]
        w_in = [jnp.exp(c) for c in cum]
        w_inv = [jnp.exp(-c) for c in cum]
        w_tot = [w[chunk - 1:chunk, :] for w in w_in]
        ar = [_split(jnp.concatenate([-kk[j] * jnp.exp(cum[j] - ld[j]), r[j] * w_in[j]], axis=0)) for j in hd]
        bk_f = [jnp.concatenate([kk[j] * alpha[j] * w_inv[j], k2[j] * w_inv[j]], axis=0) for j in hd]
        big = [_dot3(ar[j], _split(bk_f[j]), "nt") for j in hd]
        a_ab = [jnp.where(strict, b_[:chunk, :chunk], 0.0) for b_ in big]
        a_ak = [jnp.where(strict, b_[:chunk, chunk:], 0.0) for b_ in big]
        a_r = [jnp.concatenate([jnp.where(incl, b_[chunk:, :chunk], 0.0),
                                jnp.where(incl, b_[chunk:, chunk:], 0.0)], axis=1) for b_ in big]
        tinv = [eye_c + a for a in a_ab]
        p = a_ab
        for _ in range(n_double):
            ps = [_split(x) for x in p]
            p = [_dot3(x, x, "nn") for x in ps]
            tinv = [tinv[j] + _dot3(_split(tinv[j]), _split(p[j]), "nn") for j in hd]

        hdep = [_dot3(ar[j], _split(hs[j]), "nn") for j in hd]
        akv = [_dot3(_split(a_ak[j]), _split(v[j]), "nn") for j in hd]
        u = [_dot3(_split(tinv[j]), _split(hdep[j][:chunk] + akv[j]), "nn") for j in hd]
        uv = [_split(jnp.concatenate([u[j], v[j]], axis=0)) for j in hd]
        o = [hdep[j][chunk:] + _dot3(_split(a_r[j]), uv[j], "nn") for j in hd]
        w_col = [jnp.sum(eye_n * w_tot[j], axis=1, keepdims=True) for j in hd]
        h_new = [hs[j] * w_col[j] + _dot3(_split(bk_f[j] * w_tot[j]), uv[j], "tn") for j in hd]

        outs = []
        for j in hd:
            mean = jnp.mean(o[j], axis=-1, keepdims=True)
            var = jnp.mean(jnp.square(o[j] - mean), axis=-1, keepdims=True)
            on = (o[j] - mean) * lax.rsqrt(var + GN_EPS) * prm(5, j) + prm(6, j)
            bonus = jnp.sum(r[j] * k2[j] * prm(4, j), axis=-1, keepdims=True) * v[j]
            outs.append(on + bonus)
        y_ref[rows, :] = (jnp.concatenate(outs, axis=1) * g_ref[rows, :]).astype(y_ref.dtype)
        return tuple(h_new)

    hs = lax.fori_loop(0, n_chunks, one_chunk, tuple(h_scr[j] for j in range(heads)))
    for j in range(heads):
        h_scr[j] = hs[j]

    @pl.when(c_idx == pl.num_programs(2) - 1)
    def _():
        hout_ref[0] = h_scr[...]


def wkv(r, k, v, wl, al, g, prm, h0, seq_len):
    m, d = r.shape
    b = m // seq_len
    n = RWKV_HEAD
    nh = d // n
    heads = _pick(nh, (8, 4, 2, 1))
    chunk = min(64, seq_len)
    tb = _pick(seq_len, (512, 256, 128, 64, 32, 16, 8))
    nb = seq_len // tb
    wd = heads * n
    row_spec = pl.BlockSpec((tb, wd), lambda bi, hi, ci: (bi * nb + ci, hi))
    st_spec = pl.BlockSpec((1, heads, n, n), lambda bi, hi, ci: (bi, hi, 0, 0))
    return pl.pallas_call(
        functools.partial(_wkv_kernel, chunk=chunk, heads=heads),
        grid=(b, nh // heads, nb),
        in_specs=[row_spec] * 6 + [pl.BlockSpec((8, wd), lambda bi, hi, ci: (0, hi)), st_spec],
        out_specs=[row_spec, st_spec],
        out_shape=[jax.ShapeDtypeStruct((m, d), BF16), jax.ShapeDtypeStruct((b, nh, n, n), F32)],
        scratch_shapes=[pltpu.VMEM((heads, n, n), F32)],
        compiler_params=_params("parallel", "parallel", "arbitrary"),
        name="wkv",
    )(r, k, v, wl, al, g, prm, h0)


def _silu_mul(g, u):
    return g * (1.0 / (1.0 + jnp.exp(-g))) * u


def _ffn_kernel(h_ref, wg_ref, wu_ref, wd_ref, res_ref, o_ref, acc_ref):
    j = pl.program_id(1)

    @pl.when(j == 0)
    def _():
        acc_ref[...] = res_ref[...]

    h = h_ref[...]
    a = _silu_mul(_dot(h, wg_ref[...]), _dot(h, wu_ref[...])).astype(BF16)
    acc_ref[...] += _dot(a, wd_ref[...])

    @pl.when(j == pl.num_programs(1) - 1)
    def _():
        o_ref[...] = acc_ref[...]


def ffn(h, wg, wu, wd, res):
    m, d = h.shape
    f = wg.shape[1]
    tm = _pick(m, (512, 256, 128, 64, 32, 16, 8))
    tf = _pick(f, (512, 256, 128))
    return pl.pallas_call(
        _ffn_kernel,
        grid=(m // tm, f // tf),
        in_specs=[
            pl.BlockSpec((tm, d), lambda i, j: (i, 0)),
            pl.BlockSpec((d, tf), lambda i, j: (0, j)),
            pl.BlockSpec((d, tf), lambda i, j: (0, j)),
            pl.BlockSpec((tf, d), lambda i, j: (j, 0)),
            pl.BlockSpec((tm, d), lambda i, j: (i, 0)),
        ],
        out_specs=pl.BlockSpec((tm, d), lambda i, j: (i, 0)),
        out_shape=jax.ShapeDtypeStruct((m, d), F32),
        scratch_shapes=[pltpu.VMEM((tm, d), F32)],
        compiler_params=_params("parallel", "arbitrary"),
        name="ffn",
    )(h, wg, wu, wd, res)


def _moe_ffn_kernel(te_ref, tv_ref, x_ref, wg_ref, wu_ref, wd_ref, o_ref, acc_ref):
    i = pl.program_id(0)
    j = pl.program_id(1)

    @pl.when(tv_ref[i] > 0)
    def _():
        @pl.when(j == 0)
        def _():
            acc_ref[...] = jnp.zeros_like(acc_ref)

        x = x_ref[...].astype(BF16)
        a = _silu_mul(_dot(x, wg_ref[0]), _dot(x, wu_ref[0])).astype(BF16)
        acc_ref[...] += _dot(a, wd_ref[0])

        @pl.when(j == pl.num_programs(1) - 1)
        def _():
            o_ref[...] = acc_ref[...]

    @pl.when(tv_ref[i] == 0)
    def _():
        o_ref[...] = jnp.zeros_like(o_ref)


def moe_ffn(x_sorted, wg, wu, wd, tile_expert, tile_valid, tm):
    mp, d = x_sorted.shape
    f = wg.shape[2]
    tf = _pick(f, (512, 256, 128))
    nf = f // tf

    def fj(i, j, tv):
        return jnp.where(tv[i] > 0, j, nf - 1)

    grid_spec = pltpu.PrefetchScalarGridSpec(
        num_scalar_prefetch=2,
        grid=(mp // tm, nf),
        in_specs=[
            pl.BlockSpec((tm, d), lambda i, j, te, tv: (i, 0)),
            pl.BlockSpec((1, d, tf), lambda i, j, te, tv: (te[i], 0, fj(i, j, tv))),
            pl.BlockSpec((1, d, tf), lambda i, j, te, tv: (te[i], 0, fj(i, j, tv))),
            pl.BlockSpec((1, tf, d), lambda i, j, te, tv: (te[i], fj(i, j, tv), 0)),
        ],
        out_specs=pl.BlockSpec((tm, d), lambda i, j, te, tv: (i, 0)),
        scratch_shapes=[pltpu.VMEM((tm, d), F32)],
    )
    return pl.pallas_call(
        _moe_ffn_kernel,
        grid_spec=grid_spec,
        out_shape=jax.ShapeDtypeStruct((mp, d), F32),
        compiler_params=_params("parallel", "arbitrary"),
        name="moe_ffn",
    )(tile_expert, tile_valid, x_sorted, wg, wu, wd)


def _router_kernel(x_ref, g_ref, wr_ref, br_ref, h_ref, rt_ref):
    h = _rms(x_ref[...]) * g_ref[...]
    h_ref[...] = h
    logits = _dot_hi(h, wr_ref[...]) + br_ref[...]
    lane = lax.broadcasted_iota(jnp.int32, logits.shape, 1)
    m1 = jnp.max(logits, axis=-1, keepdims=True)
    i1 = jnp.min(jnp.where(logits == m1, lane, LANES), axis=-1, keepdims=True)
    rest = jnp.where(lane == i1, -jnp.inf, logits)
    m2 = jnp.max(rest, axis=-1, keepdims=True)
    i2 = jnp.min(jnp.where(rest == m2, lane, LANES), axis=-1, keepdims=True)
    e = jnp.exp(m2 - m1)
    g1 = 1.0 / (1.0 + e)
    g2 = e / (1.0 + e)
    rt_ref[...] = jnp.where(lane == 0, g1, jnp.where(lane == 1, g2, jnp.where(
        lane == 2, i1.astype(F32), jnp.where(lane == 3, i2.astype(F32), 0.0))))


def router(x, gain, w_router, b_router):
    m, d = x.shape
    e = w_router.shape[1]
    wr = jnp.pad(w_router.astype(F32), ((0, 0), (0, LANES - e)))
    br = jnp.pad(b_router.astype(F32), (0, LANES - e), constant_values=NEG_INF).reshape(1, LANES)
    tm = _pick(m, (512, 256, 128, 64, 32, 16, 8))
    return pl.pallas_call(
        _router_kernel,
        grid=(m // tm,),
        in_specs=[
            pl.BlockSpec((tm, d), lambda i: (i, 0)),
            pl.BlockSpec((1, d), lambda i: (0, 0)),
            pl.BlockSpec((d, LANES), lambda i: (0, 0)),
            pl.BlockSpec((1, LANES), lambda i: (0, 0)),
        ],
        out_specs=[pl.BlockSpec((tm, d), lambda i: (i, 0)), pl.BlockSpec((tm, LANES), lambda i: (i, 0))],
        out_shape=[jax.ShapeDtypeStruct((m, d), F32), jax.ShapeDtypeStruct((m, LANES), F32)],
        compiler_params=_params("parallel"),
        name="router",
    )(x, gain.reshape(1, d), wr, br)


def _row_copy(src_hbm, dst_ref, sem, src_row, dst_row):
    return pltpu.make_async_copy(src_hbm.at[pl.ds(src_row, 1)], dst_ref.at[pl.ds(dst_row, 1)], sem)


def _gather_rows(idx_ref, base, src_hbm, dst_ref, sem, n_rows):
    def start(r, c):
        _row_copy(src_hbm, dst_ref, sem, idx_ref[base + r], r).start()
        return c

    def wait(r, c):
        _row_copy(src_hbm, dst_ref, sem, 0, r).wait()
        return c

    lax.fori_loop(0, n_rows, start, 0)
    lax.fori_loop(0, n_rows, wait, 0)


def _gather_kernel(idx_ref, src_hbm, o_ref, sem):
    tr = o_ref.shape[0]
    _gather_rows(idx_ref, pl.program_id(0) * tr, src_hbm, o_ref, sem, tr)


def gather_rows(src, idx, tr):
    n = idx.shape[0]
    d = src.shape[1]
    grid_spec = pltpu.PrefetchScalarGridSpec(
        num_scalar_prefetch=1,
        grid=(n // tr,),
        in_specs=[pl.BlockSpec(memory_space=pl.ANY)],
        out_specs=pl.BlockSpec((tr, d), lambda i, idx: (i, 0)),
        scratch_shapes=[pltpu.SemaphoreType.DMA(())],
    )
    return pl.pallas_call(
        _gather_kernel,
        grid_spec=grid_spec,
        out_shape=jax.ShapeDtypeStruct((n, d), src.dtype),
        compiler_params=_params("arbitrary"),
        name="gather_rows",
    )(idx, src)


def _combine_kernel(p1_ref, p2_ref, y_hbm, x_ref, rt_ref, g_ref, o_ref, b1_ref, b2_ref, sem1, sem2):
    tr = x_ref.shape[0]
    base = pl.program_id(0) * tr

    def start(r, c):
        _row_copy(y_hbm, b1_ref, sem1, p1_ref[base + r], r).start()
        _row_copy(y_hbm, b2_ref, sem2, p2_ref[base + r], r).start()
        return c

    def wait(r, c):
        _row_copy(y_hbm, b1_ref, sem1, 0, r).wait()
        _row_copy(y_hbm, b2_ref, sem2, 0, r).wait()
        return c

    lax.fori_loop(0, tr, start, 0)
    lax.fori_loop(0, tr, wait, 0)
    rt = rt_ref[...]
    x = x_ref[...] + rt[:, 0:1] * b1_ref[...] + rt[:, 1:2] * b2_ref[...]
    o_ref[...] = _rms(x) * g_ref[...]


def moe_combine(x, y_sorted, rt, pos1, pos2, gain):
    m, d = x.shape
    tr = _pick(m, (128, 64, 32, 16, 8))
    grid_spec = pltpu.PrefetchScalarGridSpec(
        num_scalar_prefetch=2,
        grid=(m // tr,),
        in_specs=[
            pl.BlockSpec(memory_space=pl.ANY),
            pl.BlockSpec((tr, d), lambda i, p1, p2: (i, 0)),
            pl.BlockSpec((tr, LANES), lambda i, p1, p2: (i, 0)),
            pl.BlockSpec((1, d), lambda i, p1, p2: (0, 0)),
        ],
        out_specs=pl.BlockSpec((tr, d), lambda i, p1, p2: (i, 0)),
        scratch_shapes=[pltpu.VMEM((tr, d), F32), pltpu.VMEM((tr, d), F32),
                        pltpu.SemaphoreType.DMA(()), pltpu.SemaphoreType.DMA(())],
    )
    return pl.pallas_call(
        _combine_kernel,
        grid_spec=grid_spec,
        out_shape=jax.ShapeDtypeStruct((m, d), F32),
        compiler_params=_params("arbitrary"),
        name="moe_combine",
    )(pos1, pos2, y_sorted, x, rt, gain.reshape(1, d))


def moe_layer(x, gain_ffn, gain_final, w_router, b_router, wg, wu, wd):
    m, d = x.shape
    n_exp = wg.shape[0]
    h, rt = router(x, gain_ffn, w_router, b_router)
    tm = 512 if m * MOE_TOPK >= 8 * 512 else 64
    experts = rt[:, 2:4].astype(jnp.int32).reshape(-1)
    n_asg = experts.shape[0]
    counts = jnp.sum(experts[:, None] == jnp.arange(n_exp)[None, :], axis=0)
    padded = ((counts + tm - 1) // tm) * tm
    start = jnp.cumsum(counts) - counts
    pstart = jnp.cumsum(padded) - padded
    order = jnp.argsort(experts, stable=True)
    rank = jnp.zeros((n_asg,), jnp.int32).at[order].set(jnp.arange(n_asg, dtype=jnp.int32))
    dest = (pstart[experts] + rank - start[experts]).astype(jnp.int32)
    mp = n_asg + n_exp * tm
    src_token = jnp.zeros((mp,), jnp.int32).at[dest].set(jnp.arange(n_asg, dtype=jnp.int32) // MOE_TOPK)
    tile_start = jnp.arange(mp // tm, dtype=jnp.int32) * tm
    pend = jnp.cumsum(padded)
    tile_expert = jnp.minimum(jnp.sum(tile_start[:, None] >= pend[None, :], axis=1), n_exp - 1).astype(jnp.int32)
    tile_valid = (tile_start < pend[-1]).astype(jnp.int32)
    last_used = jnp.max(jnp.where(tile_valid > 0, tile_expert, 0))
    tile_expert = jnp.where(tile_valid > 0, tile_expert, last_used)

    x_sorted = gather_rows(h, src_token, tr=_pick(mp, (128, 64)))
    y_sorted = moe_ffn(x_sorted, wg, wu, wd, tile_expert, tile_valid, tm)
    pos = dest.reshape(m, MOE_TOPK)
    return moe_combine(x, y_sorted, rt, pos[:, 0], pos[:, 1], gain_final)


_T5_EXACT = REL_BUCKETS // 2
_T5_STEPS = [_T5_EXACT] + [
    math.ceil(_T5_EXACT * (REL_MAX_DIST / _T5_EXACT) ** (n / (REL_BUCKETS - _T5_EXACT)))
    for n in range(1, REL_BUCKETS - _T5_EXACT)]
T5_SATURATION = _T5_STEPS[-1]


def _rel_bias_of_dist(d, table):
    val = jnp.where(d >= 1, table(1), table(0))
    for b in range(2, _T5_EXACT):
        val = jnp.where(d >= b, table(b), val)
    for n, t in enumerate(_T5_STEPS):
        val = jnp.where(d >= t, table(_T5_EXACT + n), val)
    return val


def _prompt_bias_kernel(rb_ref, dg_ref, aj_ref):
    h = pl.program_id(0)
    blk = dg_ref.shape[1]
    ri = lax.broadcasted_iota(jnp.int32, (blk, blk), 0)
    ci = lax.broadcasted_iota(jnp.int32, (blk, blk), 1)
    table = lambda b: rb_ref[h, b]
    dg_ref[0] = _rel_bias_of_dist(ri - ci, table)
    aj_ref[0] = _rel_bias_of_dist(ri - ci + blk, table)


def prompt_bias(rel_bias):
    nh = rel_bias.shape[1]
    blk = MOBA_BLOCK
    tile = pl.BlockSpec((1, blk, blk), lambda h: (h, 0, 0))
    return pl.pallas_call(
        _prompt_bias_kernel,
        grid=(nh,),
        in_specs=[pl.BlockSpec(memory_space=pltpu.SMEM)],
        out_specs=[tile, tile],
        out_shape=[jax.ShapeDtypeStruct((nh, blk, blk), F32)] * 2,
        compiler_params=_params("arbitrary"),
        name="prompt_bias",
    )(rel_bias.T)


def _moba_prompt_kernel(c_last_ref, q_ref, k_ref, v_ref, dg_ref, aj_ref, o_ref, s_ref):
    h = pl.program_id(1)
    qi = pl.program_id(2)
    blk = MOBA_BLOCK
    scale = ATTN_HEAD_DIM ** -0.5
    q = q_ref[...]
    lane = lax.broadcasted_iota(jnp.int32, (blk, LANES), 1)
    c_last = c_last_ref[h]

    def rows(j):
        return pl.ds(pl.multiple_of(j * blk, blk), blk)

    def qk(j, gate):
        s = _dot_nt(q, k_ref[rows(j), :])
        s_ref[j] = s
        return jnp.where(lane == j, jnp.sum(s, axis=-1, keepdims=True), gate)

    gate = lax.fori_loop(0, qi, qk, jnp.full((blk, LANES), -jnp.inf, F32))
    cnt = jnp.zeros((blk, LANES), jnp.int32)
    for jp in range(k_ref.shape[0] // blk):
        col = gate[:, jp:jp + 1]
        cnt = cnt + jnp.where((col > gate) | ((col == gate) & (jp < lane)), 1, 0)
    sel = jnp.where((cnt < MOBA_TOPK) & (lane < qi), 1.0, 0.0)

    ri = lax.broadcasted_iota(jnp.int32, (blk, blk), 0)
    ci = lax.broadcasted_iota(jnp.int32, (blk, blk), 1)
    own = jnp.where(ri >= ci, _dot_nt(q, k_ref[rows(qi), :]) * scale + dg_ref[0], NEG_INF)

    def mask_max(j, m):
        bias = jnp.where(j == qi - 1, aj_ref[0], c_last)
        picked = jnp.sum(jnp.where(lane == j, sel, 0.0), axis=-1, keepdims=True)
        lm = jnp.where(picked > 0.0, s_ref[j] * scale + bias, NEG_INF)
        s_ref[j] = lm
        return jnp.maximum(m, jnp.max(lm, axis=-1, keepdims=True))

    m = lax.fori_loop(0, qi, mask_max, jnp.max(own, axis=-1, keepdims=True))

    def pv(j, carry):
        l, acc = carry
        p = jnp.exp(s_ref[j] - m)
        return l + jnp.sum(p, axis=-1, keepdims=True), acc + _dot(p.astype(BF16), v_ref[rows(j), :])

    p_own = jnp.exp(own - m)
    l, acc = lax.fori_loop(0, qi, pv, (jnp.sum(p_own, axis=-1, keepdims=True),
                                       _dot(p_own.astype(BF16), v_ref[rows(qi), :])))
    o_ref[...] = (acc / l).astype(o_ref.dtype)


def moba_prompt(q, k, v, rel_bias, seq_len):
    m, d = q.shape
    dh = ATTN_HEAD_DIM
    nh = d // dh
    b = m // seq_len
    blk = MOBA_BLOCK
    nq = seq_len // blk
    dg, aj = prompt_bias(rel_bias)
    kv_spec = pl.BlockSpec((seq_len, dh), lambda bi, hi, qi: (bi, hi))
    tile = pl.BlockSpec((1, blk, blk), lambda bi, hi, qi: (hi, 0, 0))
    q_spec = pl.BlockSpec((blk, dh), lambda bi, hi, qi: (bi * nq + qi, hi))
    return pl.pallas_call(
        _moba_prompt_kernel,
        grid=(b, nh, nq),
        in_specs=[pl.BlockSpec(memory_space=pltpu.SMEM), q_spec, kv_spec, kv_spec, tile, tile],
        out_specs=q_spec,
        out_shape=jax.ShapeDtypeStruct((m, d), BF16),
        scratch_shapes=[pltpu.VMEM((nq, blk, blk), F32)],
        compiler_params=_params("parallel", "parallel", "arbitrary"),
        name="moba_prompt",
    )(rel_bias[REL_BUCKETS - 1], q, k, v, dg, aj)


def _page_specs(n, page_shape):
    zeros = (0,) * len(page_shape)
    return [pl.BlockSpec((1,) + page_shape, lambda bi, p, pt, i=i: (pt[bi, n * p + i],) + zeros) for i in range(n)]


def _ksum_kernel(pt_ref, *refs, pages_per_block):
    k_refs, o_ref = refs[:-1], refs[-1]
    for blk in range(len(k_refs) // pages_per_block):
        s = jnp.sum(k_refs[blk * pages_per_block][0], axis=0)
        for i in range(1, pages_per_block):
            s = s + jnp.sum(k_refs[blk * pages_per_block + i][0], axis=0)
        o_ref[0, blk] = s


def cache_block_ksum(cache_k, page_table):
    b, n_pages = page_table.shape
    _, page, nh, dh = cache_k.shape
    ppb = MOBA_BLOCK // page
    pps = _pick(n_pages, (4 * ppb, 2 * ppb, ppb))
    grid_spec = pltpu.PrefetchScalarGridSpec(
        num_scalar_prefetch=1,
        grid=(b, n_pages // pps),
        in_specs=_page_specs(pps, (page, nh, dh)),
        out_specs=pl.BlockSpec((1, pps // ppb, nh, dh), lambda bi, p, pt: (bi, p, 0, 0)),
    )
    return pl.pallas_call(
        functools.partial(_ksum_kernel, pages_per_block=ppb),
        grid_spec=grid_spec,
        out_shape=jax.ShapeDtypeStruct((b, n_pages // ppb, nh, dh), F32),
        compiler_params=_params("parallel", "arbitrary"),
        name="cache_block_ksum",
    )(page_table, *([cache_k] * pps))


def _sample_select_kernel(q_ref, ks_ref, sel_ref):
    nh = q_ref.shape[1]
    nblk = ks_ref.shape[2]
    gate = jnp.concatenate(
        [_dot_nt(ks_ref[0, h], q_ref[0, h].astype(F32), _HI) for h in range(nh)], axis=1)
    row = lax.broadcasted_iota(jnp.int32, gate.shape, 0)
    cnt = jnp.zeros(gate.shape, jnp.int32)
    for n in range(nblk):
        g_n = gate[n:n + 1, :]
        cnt = cnt + jnp.where((g_n > gate) | ((g_n == gate) & (n < row)), 1, 0)
    sel_ref[0] = jnp.where(cnt < MOBA_TOPK, 0.0, NEG_INF)


def sample_select(q_hq, ksum_h):
    b, nh, tq, dh = q_hq.shape
    nblk = ksum_h.shape[2]
    return pl.pallas_call(
        _sample_select_kernel,
        grid=(b,),
        in_specs=[pl.BlockSpec((1, nh, tq, dh), lambda i: (i, 0, 0, 0)),
                  pl.BlockSpec((1, nh, nblk, dh), lambda i: (i, 0, 0, 0))],
        out_specs=pl.BlockSpec((1, nblk, nh * tq), lambda i: (i, 0, 0)),
        out_shape=jax.ShapeDtypeStruct((b, nblk, nh * tq), F32),
        compiler_params=_params("parallel"),
        name="sample_select",
    )(q_hq, ksum_h)


def _sample_bias_kernel(rt_ref, tiles_ref, new_ref, *, n_heads, tq, page):
    table = lambda b: rt_ref[:, b:b + 1]
    ri = lax.broadcasted_iota(jnp.int32, tiles_ref.shape[1:], 0)
    ci = lax.broadcasted_iota(jnp.int32, tiles_ref.shape[1:], 1)
    same = ci % n_heads == ri // tq
    tiles_ref[0] = jnp.where(same, table(REL_BUCKETS - 1), NEG_INF)
    tiles_ref[1] = jnp.where(same, _rel_bias_of_dist(page + ri % tq - ci // n_heads, table), NEG_INF)
    ri = lax.broadcasted_iota(jnp.int32, new_ref.shape, 0)
    ci = lax.broadcasted_iota(jnp.int32, new_ref.shape, 1)
    d = ri % tq - ci // n_heads
    ok = (d >= 0) & (ci % n_heads == ri // tq)
    new_ref[...] = jnp.where(ok, _rel_bias_of_dist(d, table), NEG_INF)


def sample_bias(rel_bias, tq, page):
    nh = rel_bias.shape[1]
    rows = nh * tq
    rt = jnp.repeat(rel_bias.T, tq, axis=0)
    return pl.pallas_call(
        functools.partial(_sample_bias_kernel, n_heads=nh, tq=tq, page=page),
        out_shape=[jax.ShapeDtypeStruct((2, rows, page * nh), F32), jax.ShapeDtypeStruct((rows, rows), F32)],
        compiler_params=pltpu.CompilerParams(vmem_limit_bytes=V7X_VMEM_LIMIT),
        name="sample_bias",
    )(rt)


def _moba_sample_kernel(pt_ref, q_ref, kn_ref, vn_ref, *rest, ppb):
    k_refs, v_refs, tile_refs = rest[:ppb], rest[ppb:2 * ppb], rest[2 * ppb:3 * ppb]
    sel_ref, new_ref, o_ref, m_ref, l_ref, acc_ref = rest[3 * ppb:]
    p = pl.program_id(1)
    scale = ATTN_HEAD_DIM ** -0.5
    q = q_ref[0]
    rows = q.shape[0]

    @pl.when(p == 0)
    def _():
        lm = _dot_nt(q, kn_ref[0]) * scale + new_ref[...]
        m = jnp.max(lm, axis=-1, keepdims=True)
        pr = jnp.exp(lm - m)
        m_ref[...] = m
        l_ref[...] = jnp.sum(pr, axis=-1, keepdims=True)
        acc_ref[...] = _dot(pr.astype(BF16), vn_ref[0])

    ri = lax.broadcasted_iota(jnp.int32, (rows, rows), 0)
    ci = lax.broadcasted_iota(jnp.int32, (rows, rows), 1)
    picked = jnp.sum(jnp.where(ri == ci, sel_ref[0, 0], 0.0), axis=-1, keepdims=True)
    lm = [_dot_nt(q, k_refs[i][0].astype(BF16)) * scale + tile_refs[i][0] + picked for i in range(ppb)]
    m_old = m_ref[...]
    m_new = m_old
    for x in lm:
        m_new = jnp.maximum(m_new, jnp.max(x, axis=-1, keepdims=True))
    alpha = jnp.exp(m_old - m_new)
    pr = [jnp.exp(x - m_new) for x in lm]
    m_ref[...] = m_new
    l_ref[...] = alpha * l_ref[...] + sum(jnp.sum(x, axis=-1, keepdims=True) for x in pr)
    acc_ref[...] = alpha * acc_ref[...] + sum(
        _dot(pr[i].astype(BF16), v_refs[i][0].astype(BF16)) for i in range(ppb))

    @pl.when(p == pl.num_programs(1) - 1)
    def _():
        o_ref[0] = acc_ref[...] / l_ref[...]


def moba_sample(q, k_new, v_new, cache_k, cache_v, page_table, rel_bias, tq):
    m, d = q.shape
    b = m // tq
    n_pages = page_table.shape[1]
    _, page, nh, dh = cache_k.shape
    ppb = MOBA_BLOCK // page
    rows = nh * tq
    past = n_pages * page
    assert MOBA_BLOCK % page == 0 and past % MOBA_BLOCK == 0 and tq <= MOBA_BLOCK and page + 1 >= T5_SATURATION

    q_hq = q.reshape(b, tq, nh, dh).transpose(0, 2, 1, 3)
    ksum = cache_block_ksum(cache_k, page_table).transpose(0, 2, 1, 3)
    nblk = past // MOBA_BLOCK
    sel = sample_select(q_hq, ksum).reshape(b, nblk, 1, rows)
    bias_tiles, bias_new = sample_bias(rel_bias, tq, page)
    q_rows = q_hq.reshape(b, rows, dh)
    kn = k_new.astype(BF16).reshape(b, rows, dh)
    vn = v_new.astype(BF16).reshape(b, rows, dh)

    cols = page * nh
    n_phys = cache_k.shape[0]
    ck = cache_k.reshape(n_phys, cols, dh)
    cv = cache_v.reshape(n_phys, cols, dh)
    small = pl.BlockSpec((1, rows, dh), lambda bi, p, pt: (bi, 0, 0))
    pages = _page_specs(ppb, (cols, dh))
    far_tile = pl.BlockSpec((1, rows, cols), lambda bi, p, pt: (0, 0, 0))
    end_tile = pl.BlockSpec((1, rows, cols), lambda bi, p, pt: (jnp.where(p == nblk - 1, 1, 0), 0, 0))
    grid_spec = pltpu.PrefetchScalarGridSpec(
        num_scalar_prefetch=1,
        grid=(b, nblk),
        in_specs=[small, small, small] + pages + pages + [far_tile] * (ppb - 1) + [end_tile]
        + [pl.BlockSpec((1, 1, 1, rows), lambda bi, p, pt: (bi, p, 0, 0)),
           pl.BlockSpec((rows, rows), lambda bi, p, pt: (0, 0))],
        out_specs=small,
        scratch_shapes=[pltpu.VMEM((rows, 1), F32), pltpu.VMEM((rows, 1), F32), pltpu.VMEM((rows, dh), F32)],
    )
    o = pl.pallas_call(
        functools.partial(_moba_sample_kernel, ppb=ppb),
        grid_spec=grid_spec,
        out_shape=jax.ShapeDtypeStruct((b, rows, dh), F32),
        compiler_params=_params("parallel", "arbitrary"),
        name="moba_sample",
    )(page_table, q_rows, kn, vn, *([ck] * ppb), *([cv] * ppb), *([bias_tiles] * ppb), sel, bias_new)
    return o.reshape(b, nh, tq, dh).transpose(0, 2, 1, 3).reshape(m, d).astype(BF16)


def rwkv_time_mix(x, seq_len, shift0, wkv0, gain, w):
    m, d = x.shape
    (xr, xw, xk, xv, xa, xg), shift_new = rwkv_mix(x, gain, w["mu"], shift0, seq_len)
    r = matmul(xr, w["w_r"])
    k = matmul(xk, w["w_k"])
    v = matmul(xv, w["w_v"])
    wl = lora(xw, w["w1"], w["w2"], "tanh")
    al = lora(xa, w["a1"], w["a2"], "none")
    g = lora(xg, w["g1"], w["g2"], "sigmoid")
    y, h_fin = wkv(r, k, v, wl, al, g, w["prm"], jnp.swapaxes(wkv0, -1, -2), seq_len)
    x_new = matmul(y, w["w_o"], res=x)
    return x_new, jnp.swapaxes(h_fin, -1, -2), shift_new


def _rwkv_weights(i, rwkv_mu, rwkv_w_r, rwkv_w_k, rwkv_w_v, rwkv_w_o, rwkv_w0, rwkv_w1, rwkv_w2, rwkv_a0, rwkv_a1,
                  rwkv_a2, rwkv_g1, rwkv_g2, rwkv_k_k, rwkv_k_a, rwkv_r_k, rwkv_gn_g, rwkv_gn_b):
    d = rwkv_mu.shape[-1]
    w1, w2 = _pad_rank(rwkv_w1[i], rwkv_w2[i])
    a1, a2 = _pad_rank(rwkv_a1[i], rwkv_a2[i])
    g1, g2 = _pad_rank(rwkv_g1[i], rwkv_g2[i])
    prm = jnp.stack([rwkv_w0[i], rwkv_a0[i], rwkv_k_k[i], rwkv_k_a[i], rwkv_r_k[i].reshape(d), rwkv_gn_g[i],
                     rwkv_gn_b[i], jnp.zeros((d,), F32)]).astype(F32)
    return dict(mu=rwkv_mu[i], w_r=rwkv_w_r[i].astype(BF16), w_k=rwkv_w_k[i].astype(BF16),
                w_v=rwkv_w_v[i].astype(BF16), w_o=rwkv_w_o[i].astype(BF16),
                w1=w1, w2=w2, a1=a1, a2=a2, g1=g1, g2=g2, prm=prm)


def kernel(x_prompt, x_sample, state_wkv, state_shift, cache_k, cache_v, page_table, norm_mix, norm_ffn, norm_kv, norm_final, rwkv_mu, rwkv_w_r, rwkv_w_k, rwkv_w_v, rwkv_w_o, rwkv_w0, rwkv_w1, rwkv_w2, rwkv_a0, rwkv_a1, rwkv_a2, rwkv_g1, rwkv_g2, rwkv_k_k, rwkv_k_a, rwkv_r_k, rwkv_gn_g, rwkv_gn_b, kv_w, attn_w_q, attn_w_o, rel_bias, ffn_w_gate, ffn_w_up, ffn_w_down, moe_w_router, moe_b_router, moe_w_gate, moe_w_up, moe_w_down):
    d = x_prompt.shape[-1]
    nh = d // ATTN_HEAD_DIM
    assert norm_mix.shape[0] == 2 and rwkv_mu.shape[0] == 1 and attn_w_q.shape[0] == 1
    assert ffn_w_gate.shape[0] == 1 and moe_w_gate.shape[0] == 1

    w_rwkv = _rwkv_weights(0, rwkv_mu, rwkv_w_r, rwkv_w_k, rwkv_w_v, rwkv_w_o, rwkv_w0, rwkv_w1, rwkv_w2, rwkv_a0,
                           rwkv_a1, rwkv_a2, rwkv_g1, rwkv_g2, rwkv_k_k, rwkv_k_a, rwkv_r_k, rwkv_gn_g, rwkv_gn_b)
    wg, wu, wd = ffn_w_gate[0].astype(BF16), ffn_w_up[0].astype(BF16), ffn_w_down[0].astype(BF16)
    kvw_k, kvw_v = kv_w[:, :d].astype(BF16), kv_w[:, d:].astype(BF16)
    wq, wo = attn_w_q[0].astype(BF16), attn_w_o[0].astype(BF16)
    eg, eu, ed = moe_w_gate[0].astype(BF16), moe_w_up[0].astype(BF16), moe_w_down[0].astype(BF16)
    gains_kvq = jnp.stack([norm_kv, norm_mix[1]])

    def run_group(x3, wkv0, shift0, attend):
        b, t, _ = x3.shape
        x = x3.reshape(b * t, d)
        x, wkv_new, shift_new = rwkv_time_mix(x, t, shift0, wkv0, norm_mix[0], w_rwkv)
        (h,) = rmsnorm_multi(x, norm_ffn[0:1], [BF16])
        x = ffn(h, wg, wu, wd, x)
        h_kv, h_q = rmsnorm_multi(x, gains_kvq, [BF16, BF16])
        k = matmul(h_kv, kvw_k)
        v = matmul(h_kv, kvw_v)
        q = matmul(h_q, wq, out_dtype=BF16)
        o = attend(q, k, v, t)
        x = matmul(o, wo, res=x)
        y = moe_layer(x, norm_ffn[1], norm_final, moe_w_router[0], moe_b_router[0], eg, eu, ed)
        kv_shape = (b, t, nh, ATTN_HEAD_DIM)
        return y.reshape(b, t, d), wkv_new[None], shift_new[None], k.reshape(kv_shape), v.reshape(kv_shape)

    bp = x_prompt.shape[0]
    wkv_zero = jnp.zeros((bp,) + state_wkv.shape[2:], state_wkv.dtype)
    shift_zero = jnp.zeros((bp, d), state_shift.dtype)
    out_p = run_group(
        x_prompt, wkv_zero, shift_zero,
        lambda q, k, v, t: moba_prompt(q, k.astype(BF16), v.astype(BF16), rel_bias, t))
    out_s = run_group(
        x_sample, state_wkv[0], state_shift[0],
        lambda q, k, v, t: moba_sample(q, k, v, cache_k, cache_v, page_table, rel_bias, t))
    return (out_p[0], out_s[0]) + out_p[1:] + out_s[1:]
```

```python
import functools
import math

import jax
import jax.numpy as jnp
from jax import lax
from jax.experimental import pallas as pl
from jax.experimental.pallas import tpu as pltpu

F32 = jnp.float32
BF16 = jnp.bfloat16

RWKV_HEAD = 64
ATTN_HEAD_DIM = 128
MOBA_BLOCK = 256
MOBA_TOPK = 3
REL_BUCKETS = 32
REL_MAX_DIST = 128
MOE_TOPK = 2
GN_EPS = 64e-5
RMS_EPS = 1e-6
NEG_INF = -1e30

V7X_VMEM_LIMIT = 48 * 1024 * 1024
LANES = 128

_HI = lax.Precision.HIGHEST


def _params(*sem):
    return pltpu.CompilerParams(dimension_semantics=sem, vmem_limit_bytes=V7X_VMEM_LIMIT)


def _dot(a, b):
    return jnp.dot(a, b, preferred_element_type=F32)


def _dot_hi(a, b):
    return jnp.dot(a, b, preferred_element_type=F32, precision=_HI)


def _dot_nt(a, b, precision=None):
    return lax.dot_general(a, b, (((1,), (1,)), ((), ())), preferred_element_type=F32, precision=precision)


def _pick(n, prefs):
    for p in prefs:
        if n % p == 0:
            return p
    return n


def _mm_kernel(x_ref, w_ref, *rest, has_res):
    if has_res:
        res_ref, o_ref = rest
    else:
        (o_ref,) = rest
    acc = _dot(x_ref[...], w_ref[...])
    if has_res:
        acc = acc + res_ref[...]
    o_ref[...] = acc.astype(o_ref.dtype)


def matmul(x, w, res=None, out_dtype=F32):
    m, k = x.shape
    n = w.shape[1]
    tm = _pick(m, (1024, 512, 256, 128, 64, 32, 16, 8))
    tn = _pick(n, (512, 256, 128))
    in_specs = [pl.BlockSpec((tm, k), lambda i, j: (i, 0)), pl.BlockSpec((k, tn), lambda i, j: (0, j))]
    args = [x, w]
    if res is not None:
        in_specs.append(pl.BlockSpec((tm, tn), lambda i, j: (i, j)))
        args.append(res)
    return pl.pallas_call(
        functools.partial(_mm_kernel, has_res=res is not None),
        grid=(m // tm, n // tn),
        in_specs=in_specs,
        out_specs=pl.BlockSpec((tm, tn), lambda i, j: (i, j)),
        out_shape=jax.ShapeDtypeStruct((m, n), out_dtype),
        compiler_params=_params("parallel", "arbitrary"),
        name="matmul",
    )(*args)


def _rms(xf):
    return xf * lax.rsqrt(jnp.mean(xf * xf, axis=-1, keepdims=True) + RMS_EPS)


def _norm_kernel(x_ref, g_ref, *o_refs):
    y = _rms(x_ref[...])
    for j, o_ref in enumerate(o_refs):
        o_ref[...] = (y * g_ref[j:j + 1, :]).astype(o_ref.dtype)


def rmsnorm_multi(x, gains, out_dtypes):
    m, d = x.shape
    n = gains.shape[0]
    tm = _pick(m, (512, 256, 128, 64, 32, 16, 8))
    return pl.pallas_call(
        _norm_kernel,
        grid=(m // tm,),
        in_specs=[pl.BlockSpec((tm, d), lambda i: (i, 0)), pl.BlockSpec((n, d), lambda i: (0, 0))],
        out_specs=[pl.BlockSpec((tm, d), lambda i: (i, 0)) for _ in range(n)],
        out_shape=[jax.ShapeDtypeStruct((m, d), dt) for dt in out_dtypes],
        compiler_params=_params("parallel"),
        name="rmsnorm",
    )(x, gains)


def _mix_kernel(x_ref, g_ref, mu_ref, s0_ref, *rest, blocks_per_seq):
    o_refs, last_ref, carry_ref = rest[:6], rest[6], rest[7]
    i = pl.program_id(0)
    h = _rms(x_ref[...]) * g_ref[...]
    tm = h.shape[0]

    @pl.when(i % blocks_per_seq == 0)
    def _():
        carry_ref[...] = s0_ref[0]

    row = lax.broadcasted_iota(jnp.int32, h.shape, 0)
    prev = jnp.where(row == 0, carry_ref[...], pltpu.roll(h, 1, 0))
    xx = prev - h
    for j in range(6):
        o_refs[j][...] = (h + xx * mu_ref[j:j + 1, :]).astype(o_refs[j].dtype)
    last = h[tm - 1:tm, :]
    carry_ref[...] = last
    last_ref[0] = last


def rwkv_mix(x, gain, mu, shift0, seq_len):
    m, d = x.shape
    b = m // seq_len
    tm = _pick(seq_len, (256, 128, 64, 32, 16, 8))
    bps = seq_len // tm
    outs = pl.pallas_call(
        functools.partial(_mix_kernel, blocks_per_seq=bps),
        grid=(m // tm,),
        in_specs=[
            pl.BlockSpec((tm, d), lambda i: (i, 0)),
            pl.BlockSpec((1, d), lambda i: (0, 0)),
            pl.BlockSpec((6, d), lambda i: (0, 0)),
            pl.BlockSpec((1, 1, d), lambda i: (i // bps, 0, 0)),
        ],
        out_specs=[pl.BlockSpec((tm, d), lambda i: (i, 0)) for _ in range(6)]
        + [pl.BlockSpec((1, 1, d), lambda i: (i // bps, 0, 0))],
        out_shape=[jax.ShapeDtypeStruct((m, d), BF16) for _ in range(6)]
        + [jax.ShapeDtypeStruct((b, 1, d), F32)],
        scratch_shapes=[pltpu.VMEM((1, d), F32)],
        compiler_params=_params("arbitrary"),
        name="rwkv_mix",
    )(x, gain.reshape(1, d), mu, shift0.reshape(b, 1, d))
    return outs[:6], outs[6].reshape(b, d)


def _lora_kernel(x_ref, w1_ref, w2_ref, o_ref, *, act):
    t = _dot(x_ref[...], w1_ref[...])
    if act == "tanh":
        t = jnp.tanh(t)
    elif act == "sigmoid":
        t = 1.0 / (1.0 + jnp.exp(-t))
    o_ref[...] = _dot(t.astype(BF16), w2_ref[...])


def lora(x, w1, w2, act):
    m, d = x.shape
    r = w1.shape[1]
    n = w2.shape[1]
    tm = _pick(m, (512, 256, 128, 64, 32, 16, 8))
    return pl.pallas_call(
        functools.partial(_lora_kernel, act=act),
        grid=(m // tm,),
        in_specs=[
            pl.BlockSpec((tm, d), lambda i: (i, 0)),
            pl.BlockSpec((d, r), lambda i: (0, 0)),
            pl.BlockSpec((r, n), lambda i: (0, 0)),
        ],
        out_specs=pl.BlockSpec((tm, n), lambda i: (i, 0)),
        out_shape=jax.ShapeDtypeStruct((m, n), F32),
        compiler_params=_params("parallel"),
        name="lora",
    )(x, w1, w2)


def _pad_rank(w1, w2):
    r = w1.shape[1]
    rp = -(-r // LANES) * LANES
    if rp != r:
        w1 = jnp.pad(w1, ((0, 0), (0, rp - r)))
        w2 = jnp.pad(w2, ((0, rp - r), (0, 0)))
    return w1.astype(BF16), w2.astype(BF16)


def _split(a):
    hi = a.astype(BF16)
    return hi, (a - hi.astype(F32)).astype(BF16)


def _dot3(a, b, mode):
    (ah, al), (bh, bl) = a, b
    la = 0 if mode == "tn" else 1
    lb = 1 if mode == "nt" else 0
    lhs = jnp.concatenate([ah, ah, al], axis=la)
    rhs = jnp.concatenate([bh, bl, bh], axis=lb)
    return lax.dot_general(lhs, rhs, (((la,), (lb,)), ((), ())), preferred_element_type=F32)


def _wkv_kernel(r_ref, k_ref, v_ref, wl_ref, al_ref, g_ref, prm_ref, h0_ref, y_ref, hout_ref, h_scr,
                *, chunk, pairs):
    c_idx = pl.program_id(2)
    n = RWKV_HEAD
    pw = 2 * n
    cw = 2 * chunk
    tb = r_ref.shape[0]
    n_chunks = tb // chunk

    def iota(shape, dim):
        return lax.broadcasted_iota(jnp.int32, shape, dim)

    def block_diag(x, w):
        even = iota((1, 2 * w), 1) < w
        return jnp.concatenate([jnp.where(even, x, 0.0), jnp.where(even, 0.0, x)], axis=0)

    def split_bd(x, w):
        even = jnp.where(iota((1, 2 * w), 1) < w, 1.0, 0.0).astype(BF16)
        hi, lo = _split(x)
        return tuple(jnp.concatenate([t * even, t * (1 - even)], axis=0) for t in (hi, lo))

    def cat_rows(a, b):
        return tuple(jnp.concatenate([s, t], axis=0) for s, t in zip(a, b))

    @pl.when(c_idx == 0)
    def _():
        for p in range(pairs):
            h_scr[p] = block_diag(jnp.concatenate([h0_ref[0, 2 * p], h0_ref[0, 2 * p + 1]], axis=1), n)

    ri = iota((chunk, cw), 0)
    ci = iota((chunk, cw), 1) % chunk
    strict = ri > ci
    incl = ri >= ci
    eye_c = jnp.where(ri == ci, 1.0, 0.0)
    tri = jnp.where(iota((chunk, chunk), 0) >= iota((chunk, chunk), 1), 1.0, 0.0).astype(BF16)
    tri3 = jnp.concatenate([tri, tri, tri], axis=1)
    same_head = iota((pw, pw), 0) // n == iota((pw, pw), 1) // n
    seg = jnp.where(same_head, 1.0, 0.0).astype(BF16)
    seg2 = jnp.concatenate([seg, seg], axis=0)
    eye_pw = jnp.where(iota((pw, pw), 0) == iota((pw, pw), 1), 1.0, 0.0)
    n_double = max(int(math.log2(chunk)) - 1, 0)

    def head_sum(x):
        x_h, x_l = _split(x)
        return _dot(jnp.concatenate([x_h, x_l], axis=1), seg2)

    def one_chunk(ci_, hs):
        rows = pl.ds(pl.multiple_of(ci_ * chunk, chunk), chunk)
        r_all = r_ref[rows, :]
        k_all = k_ref[rows, :]
        v_all = v_ref[rows, :]
        wl_all = wl_ref[rows, :]
        al_all = al_ref[rows, :]
        pr = range(pairs)
        sls = [slice(j * pw, (j + 1) * pw) for j in pr]
        prm = lambda row, j: prm_ref[row:row + 1, sls[j]]
        r = [r_all[:, sl] for sl in sls]
        k = [k_all[:, sl] for sl in sls]
        v = [v_all[:, sl] for sl in sls]
        z = [-(prm(0, j) + wl_all[:, sls[j]]) for j in pr]
        ld = [-jnp.exp(-(jnp.maximum(zj, 0.0) + jnp.log(1.0 + jnp.exp(-jnp.abs(zj)))) - 0.5) for zj in z]
        alpha = [1.0 / (1.0 + jnp.exp(-(prm(1, j) + al_all[:, sls[j]]))) for j in pr]
        kkv = [k[j] * prm(2, j) for j in pr]
        kk = [kkv[j] / jnp.maximum(jnp.sqrt(head_sum(kkv[j] * kkv[j])), 1e-12) for j in pr]
        k2 = [k[j] * (1.0 + (alpha[j] - 1.0) * prm(3, j)) for j in pr]

        def ld3(x):
            x_h, x_m = _split(x)
            return jnp.concatenate([x_h, x_m, (x - x_h.astype(F32) - x_m.astype(F32)).astype(BF16)], axis=0)

        cum = [_dot(tri3, ld3(ld[j])) for j in pr]
        w_in = [jnp.exp(c) for c in cum]
        w_inv = [jnp.exp(-c) for c in cum]
        w_tot = [w[chunk - 1:chunk, :] for w in w_in]
        ar = [_split(jnp.concatenate([-kk[j] * jnp.exp(cum[j] - ld[j]), r[j] * w_in[j]], axis=0)) for j in pr]
        b_t = [kk[j] * alpha[j] * w_inv[j] for j in pr]
        k_t = [k2[j] * w_inv[j] for j in pr]
        bk_bd = [cat_rows(split_bd(b_t[j], n), split_bd(k_t[j], n)) for j in pr]
        big = [_dot3(ar[j], bk_bd[j], "nt") for j in pr]
        a_ab = [jnp.where(strict, b_[:chunk, :cw], 0.0) for b_ in big]
        a_ak = [jnp.where(strict, b_[:chunk, cw:], 0.0) for b_ in big]
        a_r = [jnp.concatenate([jnp.where(incl, b_[chunk:, :cw], 0.0),
                                jnp.where(incl, b_[chunk:, cw:], 0.0)], axis=1) for b_ in big]
        tinv = [eye_c + a for a in a_ab]
        p = a_ab
        for _ in range(n_double):
            p = [_dot3(_split(x), split_bd(x, chunk), "nn") for x in p]
            tinv = [tinv[j] + _dot3(_split(tinv[j]), split_bd(p[j], chunk), "nn") for j in pr]

        hdep = [_dot3(ar[j], _split(hs[j]), "nn") for j in pr]
        v_bd = [split_bd(v[j], n) for j in pr]
        akv = [_dot3(_split(a_ak[j]), v_bd[j], "nn") for j in pr]
        u = [_dot3(_split(tinv[j]), split_bd(hdep[j][:chunk] + akv[j], n), "nn") for j in pr]
        uv_bd = [cat_rows(split_bd(u[j], n), v_bd[j]) for j in pr]
        o = [hdep[j][chunk:] + _dot3(_split(a_r[j]), uv_bd[j], "nn") for j in pr]
        w_col = [jnp.sum(eye_pw * w_tot[j], axis=1, keepdims=True) for j in pr]
        bk_s = [_split(jnp.concatenate([b_t[j], k_t[j]], axis=0) * w_tot[j]) for j in pr]
        uv = [_split(jnp.concatenate([u[j], v[j]], axis=0)) for j in pr]
        h_new = [hs[j] * w_col[j] + jnp.where(same_head, _dot3(bk_s[j], uv[j], "tn"), 0.0) for j in pr]

        outs = []
        for j in pr:
            d_ = o[j] - head_sum(o[j]) * (1.0 / n)
            var = head_sum(d_ * d_) * (1.0 / n)
            on = d_ * lax.rsqrt(var + GN_EPS) * prm(5, j) + prm(6, j)
            outs.append(on + head_sum(r[j] * k2[j] * prm(4, j)) * v[j])
        y_ref[rows, :] = (jnp.concatenate(outs, axis=1) * g_ref[rows, :]).astype(y_ref.dtype)
        return tuple(h_new)

    hs = lax.fori_loop(0, n_chunks, one_chunk, tuple(h_scr[j] for j in range(pairs)))
    for j in range(pairs):
        h_scr[j] = hs[j]

    @pl.when(c_idx == pl.num_programs(2) - 1)
    def _():
        for j in range(pairs):
            hout_ref[0, 2 * j] = hs[j][:n, :n]
            hout_ref[0, 2 * j + 1] = hs[j][n:, n:]


def wkv(r, k, v, wl, al, g, prm, h0, seq_len):
    m, d = r.shape
    b = m // seq_len
    n = RWKV_HEAD
    nh = d // n
    heads = _pick(nh, (16, 8, 4, 2))
    assert nh % heads == 0 and heads % 2 == 0
    chunk = min(64, seq_len)
    tb = _pick(seq_len, (512, 256, 128, 64, 32, 16, 8))
    nb = seq_len // tb
    wd = heads * n
    row_spec = pl.BlockSpec((tb, wd), lambda bi, hi, ci: (bi * nb + ci, hi))
    st_spec = pl.BlockSpec((1, heads, n, n), lambda bi, hi, ci: (bi, hi, 0, 0))
    return pl.pallas_call(
        functools.partial(_wkv_kernel, chunk=chunk, pairs=heads // 2),
        grid=(b, nh // heads, nb),
        in_specs=[row_spec] * 6 + [pl.BlockSpec((8, wd), lambda bi, hi, ci: (0, hi)), st_spec],
        out_specs=[row_spec, st_spec],
        out_shape=[jax.ShapeDtypeStruct((m, d), BF16), jax.ShapeDtypeStruct((b, nh, n, n), F32)],
        scratch_shapes=[pltpu.VMEM((heads // 2, 2 * n, 2 * n), F32)],
        compiler_params=_params("parallel", "parallel", "arbitrary"),
        name="wkv",
    )(r, k, v, wl, al, g, prm, h0)


def _silu_mul(g, u):
    return g * (1.0 / (1.0 + jnp.exp(-g))) * u


def _ffn_kernel(h_ref, wg_ref, wu_ref, wd_ref, res_ref, o_ref, acc_ref):
    j = pl.program_id(1)

    @pl.when(j == 0)
    def _():
        acc_ref[...] = res_ref[...]

    h = h_ref[...]
    a = _silu_mul(_dot(h, wg_ref[...]), _dot(h, wu_ref[...])).astype(BF16)
    acc_ref[...] += _dot(a, wd_ref[...])

    @pl.when(j == pl.num_programs(1) - 1)
    def _():
        o_ref[...] = acc_ref[...]


def ffn(h, wg, wu, wd, res):
    m, d = h.shape
    f = wg.shape[1]
    tm = _pick(m, (512, 256, 128, 64, 32, 16, 8))
    tf = _pick(f, (512, 256, 128))
    return pl.pallas_call(
        _ffn_kernel,
        grid=(m // tm, f // tf),
        in_specs=[
            pl.BlockSpec((tm, d), lambda i, j: (i, 0)),
            pl.BlockSpec((d, tf), lambda i, j: (0, j)),
            pl.BlockSpec((d, tf), lambda i, j: (0, j)),
            pl.BlockSpec((tf, d), lambda i, j: (j, 0)),
            pl.BlockSpec((tm, d), lambda i, j: (i, 0)),
        ],
        out_specs=pl.BlockSpec((tm, d), lambda i, j: (i, 0)),
        out_shape=jax.ShapeDtypeStruct((m, d), F32),
        scratch_shapes=[pltpu.VMEM((tm, d), F32)],
        compiler_params=_params("parallel", "arbitrary"),
        name="ffn",
    )(h, wg, wu, wd, res)


def _moe_ffn_kernel(te_ref, tv_ref, x_ref, wg_ref, wu_ref, wd_ref, o_ref, acc_ref):
    i = pl.program_id(0)
    j = pl.program_id(1)

    @pl.when(tv_ref[i] > 0)
    def _():
        @pl.when(j == 0)
        def _():
            acc_ref[...] = jnp.zeros_like(acc_ref)

        x = x_ref[...].astype(BF16)
        a = _silu_mul(_dot(x, wg_ref[0]), _dot(x, wu_ref[0])).astype(BF16)
        acc_ref[...] += _dot(a, wd_ref[0])

        @pl.when(j == pl.num_programs(1) - 1)
        def _():
            o_ref[...] = acc_ref[...]

    @pl.when(tv_ref[i] == 0)
    def _():
        o_ref[...] = jnp.zeros_like(o_ref)


def moe_ffn(x_sorted, wg, wu, wd, tile_expert, tile_valid, tm):
    mp, d = x_sorted.shape
    f = wg.shape[2]
    tf = _pick(f, (512, 256, 128))
    nf = f // tf

    def fj(i, j, tv):
        return jnp.where(tv[i] > 0, j, nf - 1)

    grid_spec = pltpu.PrefetchScalarGridSpec(
        num_scalar_prefetch=2,
        grid=(mp // tm, nf),
        in_specs=[
            pl.BlockSpec((tm, d), lambda i, j, te, tv: (i, 0)),
            pl.BlockSpec((1, d, tf), lambda i, j, te, tv: (te[i], 0, fj(i, j, tv))),
            pl.BlockSpec((1, d, tf), lambda i, j, te, tv: (te[i], 0, fj(i, j, tv))),
            pl.BlockSpec((1, tf, d), lambda i, j, te, tv: (te[i], fj(i, j, tv), 0)),
        ],
        out_specs=pl.BlockSpec((tm, d), lambda i, j, te, tv: (i, 0)),
        scratch_shapes=[pltpu.VMEM((tm, d), F32)],
    )
    return pl.pallas_call(
        _moe_ffn_kernel,
        grid_spec=grid_spec,
        out_shape=jax.ShapeDtypeStruct((mp, d), F32),
        compiler_params=_params("parallel", "arbitrary"),
        name="moe_ffn",
    )(tile_expert, tile_valid, x_sorted, wg, wu, wd)


def _router_kernel(x_ref, g_ref, wr_ref, br_ref, h_ref, rt_ref):
    h = _rms(x_ref[...]) * g_ref[...]
    h_ref[...] = h
    logits = _dot_hi(h, wr_ref[...]) + br_ref[...]
    lane = lax.broadcasted_iota(jnp.int32, logits.shape, 1)
    m1 = jnp.max(logits, axis=-1, keepdims=True)
    i1 = jnp.min(jnp.where(logits == m1, lane, LANES), axis=-1, keepdims=True)
    rest = jnp.where(lane == i1, -jnp.inf, logits)
    m2 = jnp.max(rest, axis=-1, keepdims=True)
    i2 = jnp.min(jnp.where(rest == m2, lane, LANES), axis=-1, keepdims=True)
    e = jnp.exp(m2 - m1)
    g1 = 1.0 / (1.0 + e)
    g2 = e / (1.0 + e)
    rt_ref[...] = jnp.where(lane == 0, g1, jnp.where(lane == 1, g2, jnp.where(
        lane == 2, i1.astype(F32), jnp.where(lane == 3, i2.astype(F32), 0.0))))


def router(x, gain, w_router, b_router):
    m, d = x.shape
    e = w_router.shape[1]
    wr = jnp.pad(w_router.astype(F32), ((0, 0), (0, LANES - e)))
    br = jnp.pad(b_router.astype(F32), (0, LANES - e), constant_values=NEG_INF).reshape(1, LANES)
    tm = _pick(m, (512, 256, 128, 64, 32, 16, 8))
    return pl.pallas_call(
        _router_kernel,
        grid=(m // tm,),
        in_specs=[
            pl.BlockSpec((tm, d), lambda i: (i, 0)),
            pl.BlockSpec((1, d), lambda i: (0, 0)),
            pl.BlockSpec((d, LANES), lambda i: (0, 0)),
            pl.BlockSpec((1, LANES), lambda i: (0, 0)),
        ],
        out_specs=[pl.BlockSpec((tm, d), lambda i: (i, 0)), pl.BlockSpec((tm, LANES), lambda i: (i, 0))],
        out_shape=[jax.ShapeDtypeStruct((m, d), F32), jax.ShapeDtypeStruct((m, LANES), F32)],
        compiler_params=_params("parallel"),
        name="router",
    )(x, gain.reshape(1, d), wr, br)


def _row_copy(src_hbm, dst_ref, sem, src_row, dst_row):
    return pltpu.make_async_copy(src_hbm.at[pl.ds(src_row, 1)], dst_ref.at[pl.ds(dst_row, 1)], sem)


def _gather_rows(idx_ref, base, src_hbm, dst_ref, sem, n_rows):
    def start(r, c):
        _row_copy(src_hbm, dst_ref, sem, idx_ref[base + r], r).start()
        return c

    def wait(r, c):
        _row_copy(src_hbm, dst_ref, sem, 0, r).wait()
        return c

    lax.fori_loop(0, n_rows, start, 0)
    lax.fori_loop(0, n_rows, wait, 0)


def _gather_kernel(idx_ref, src_hbm, o_ref, sem):
    tr = o_ref.shape[0]
    _gather_rows(idx_ref, pl.program_id(0) * tr, src_hbm, o_ref, sem, tr)


def gather_rows(src, idx, tr):
    n = idx.shape[0]
    d = src.shape[1]
    grid_spec = pltpu.PrefetchScalarGridSpec(
        num_scalar_prefetch=1,
        grid=(n // tr,),
        in_specs=[pl.BlockSpec(memory_space=pl.ANY)],
        out_specs=pl.BlockSpec((tr, d), lambda i, idx: (i, 0)),
        scratch_shapes=[pltpu.SemaphoreType.DMA(())],
    )
    return pl.pallas_call(
        _gather_kernel,
        grid_spec=grid_spec,
        out_shape=jax.ShapeDtypeStruct((n, d), src.dtype),
        compiler_params=_params("arbitrary"),
        name="gather_rows",
    )(idx, src)


def _combine_kernel(p1_ref, p2_ref, y_hbm, x_ref, rt_ref, g_ref, o_ref, b1_ref, b2_ref, sem1, sem2):
    tr = x_ref.shape[0]
    base = pl.program_id(0) * tr

    def start(r, c):
        _row_copy(y_hbm, b1_ref, sem1, p1_ref[base + r], r).start()
        _row_copy(y_hbm, b2_ref, sem2, p2_ref[base + r], r).start()
        return c

    def wait(r, c):
        _row_copy(y_hbm, b1_ref, sem1, 0, r).wait()
        _row_copy(y_hbm, b2_ref, sem2, 0, r).wait()
        return c

    lax.fori_loop(0, tr, start, 0)
    lax.fori_loop(0, tr, wait, 0)
    rt = rt_ref[...]
    x = x_ref[...] + rt[:, 0:1] * b1_ref[...] + rt[:, 1:2] * b2_ref[...]
    o_ref[...] = _rms(x) * g_ref[...]


def moe_combine(x, y_sorted, rt, pos1, pos2, gain):
    m, d = x.shape
    tr = _pick(m, (128, 64, 32, 16, 8))
    grid_spec = pltpu.PrefetchScalarGridSpec(
        num_scalar_prefetch=2,
        grid=(m // tr,),
        in_specs=[
            pl.BlockSpec(memory_space=pl.ANY),
            pl.BlockSpec((tr, d), lambda i, p1, p2: (i, 0)),
            pl.BlockSpec((tr, LANES), lambda i, p1, p2: (i, 0)),
            pl.BlockSpec((1, d), lambda i, p1, p2: (0, 0)),
        ],
        out_specs=pl.BlockSpec((tr, d), lambda i, p1, p2: (i, 0)),
        scratch_shapes=[pltpu.VMEM((tr, d), F32), pltpu.VMEM((tr, d), F32),
                        pltpu.SemaphoreType.DMA(()), pltpu.SemaphoreType.DMA(())],
    )
    return pl.pallas_call(
        _combine_kernel,
        grid_spec=grid_spec,
        out_shape=jax.ShapeDtypeStruct((m, d), F32),
        compiler_params=_params("arbitrary"),
        name="moe_combine",
    )(pos1, pos2, y_sorted, x, rt, gain.reshape(1, d))


def moe_layer(x, gain_ffn, gain_final, w_router, b_router, wg, wu, wd):
    m, d = x.shape
    n_exp = wg.shape[0]
    h, rt = router(x, gain_ffn, w_router, b_router)
    tm = 512 if m * MOE_TOPK >= 8 * 512 else 64
    experts = rt[:, 2:4].astype(jnp.int32).reshape(-1)
    n_asg = experts.shape[0]
    counts = jnp.sum(experts[:, None] == jnp.arange(n_exp)[None, :], axis=0)
    padded = ((counts + tm - 1) // tm) * tm
    start = jnp.cumsum(counts) - counts
    pstart = jnp.cumsum(padded) - padded
    order = jnp.argsort(experts, stable=True)
    rank = jnp.zeros((n_asg,), jnp.int32).at[order].set(jnp.arange(n_asg, dtype=jnp.int32))
    dest = (pstart[experts] + rank - start[experts]).astype(jnp.int32)
    mp = n_asg + n_exp * tm
    src_token = jnp.zeros((mp,), jnp.int32).at[dest].set(jnp.arange(n_asg, dtype=jnp.int32) // MOE_TOPK)
    tile_start = jnp.arange(mp // tm, dtype=jnp.int32) * tm
    pend = jnp.cumsum(padded)
    tile_expert = jnp.minimum(jnp.sum(tile_start[:, None] >= pend[None, :], axis=1), n_exp - 1).astype(jnp.int32)
    tile_valid = (tile_start < pend[-1]).astype(jnp.int32)
    last_used = jnp.max(jnp.where(tile_valid > 0, tile_expert, 0))
    tile_expert = jnp.where(tile_valid > 0, tile_expert, last_used)

    x_sorted = gather_rows(h, src_token, tr=_pick(mp, (128, 64)))
    y_sorted = moe_ffn(x_sorted, wg, wu, wd, tile_expert, tile_valid, tm)
    pos = dest.reshape(m, MOE_TOPK)
    return moe_combine(x, y_sorted, rt, pos[:, 0], pos[:, 1], gain_final)


_T5_EXACT = REL_BUCKETS // 2
_T5_STEPS = [_T5_EXACT] + [
    math.ceil(_T5_EXACT * (REL_MAX_DIST / _T5_EXACT) ** (n / (REL_BUCKETS - _T5_EXACT)))
    for n in range(1, REL_BUCKETS - _T5_EXACT)]
T5_SATURATION = _T5_STEPS[-1]


def _rel_bias_of_dist(d, table):
    val = jnp.where(d >= 1, table(1), table(0))
    for b in range(2, _T5_EXACT):
        val = jnp.where(d >= b, table(b), val)
    for n, t in enumerate(_T5_STEPS):
        val = jnp.where(d >= t, table(_T5_EXACT + n), val)
    return val


def _prompt_bias_kernel(rb_ref, dg_ref, aj_ref):
    h = pl.program_id(0)
    blk = dg_ref.shape[1]
    ri = lax.broadcasted_iota(jnp.int32, (blk, blk), 0)
    ci = lax.broadcasted_iota(jnp.int32, (blk, blk), 1)
    table = lambda b: rb_ref[h, b]
    far = table(REL_BUCKETS - 1)
    dg_ref[0] = jnp.where(ri >= ci, _rel_bias_of_dist(ri - ci, table) - far, NEG_INF)
    aj_ref[0] = _rel_bias_of_dist(ri - ci + blk, table) - far


def prompt_bias(rel_bias):
    nh = rel_bias.shape[1]
    blk = MOBA_BLOCK
    tile = pl.BlockSpec((1, blk, blk), lambda h: (h, 0, 0))
    return pl.pallas_call(
        _prompt_bias_kernel,
        grid=(nh,),
        in_specs=[pl.BlockSpec(memory_space=pltpu.SMEM)],
        out_specs=[tile, tile],
        out_shape=[jax.ShapeDtypeStruct((nh, blk, blk), F32)] * 2,
        compiler_params=_params("arbitrary"),
        name="prompt_bias",
    )(rel_bias.T)


_GATE_ROWS = 16


def _moba_prompt_kernel(c_last_ref, q_ref, k_ref, v_ref, dg_ref, aj_ref, o_ref, s_ref, ks_ref):
    h = pl.program_id(1)
    qi = pl.program_id(2)
    blk = MOBA_BLOCK
    nblk = k_ref.shape[0] // blk
    scale = ATTN_HEAD_DIM ** -0.5
    q = q_ref[...]
    c_last = c_last_ref[h]

    @pl.when(qi == 0)
    def _():
        ks_ref[...] = jnp.zeros_like(ks_ref)
        for j in range(nblk):
            ks_ref[j:j + 1, :] = jnp.sum(k_ref[j * blk:(j + 1) * blk, :].astype(F32), axis=0, keepdims=True)

    half_rows = _GATE_ROWS // 2
    ks_hi, ks_lo = _split(ks_ref[...])
    gate = (_dot_nt(ks_hi, q) + _dot_nt(ks_lo, q))[:half_rows]
    row = lax.broadcasted_iota(jnp.int32, gate.shape, 0)
    gate = jnp.where(row < qi, gate, -jnp.inf)
    cnt = jnp.zeros(gate.shape, jnp.int32)
    for n in range(nblk):
        g_n = gate[n:n + 1, :]
        cnt = cnt + jnp.where((g_n > gate) | ((g_n == gate) & (n < row)), 1, 0)
    attend = ((cnt < MOBA_TOPK) & (row < qi)) | (row == qi)
    c_full = jnp.full(gate.shape, c_last, F32)
    c_hi = c_full.astype(BF16).astype(F32)
    mask_t = jnp.concatenate([jnp.where(attend, c_hi, NEG_INF), jnp.where(attend, c_full - c_hi, 0.0)],
                             axis=0).astype(BF16)

    def attend_blocks(nk):
        tk = nk * blk
        kb = lax.broadcasted_iota(jnp.int32, (_GATE_ROWS, tk), 1) // blk
        rb = lax.broadcasted_iota(jnp.int32, (_GATE_ROWS, tk), 0) % half_rows
        expand = jnp.where(kb == rb, 1.0, 0.0).astype(BF16)
        mask = lax.dot_general(mask_t, expand, (((0,), (0,)), ((), ())), preferred_element_type=F32)
        lm = _dot_nt(q, k_ref[0:tk, :]) * scale + mask
        for j in range(nk):
            s_ref[j] = lm[:, j * blk:(j + 1) * blk]
        s_ref[qi] += dg_ref[0]

        @pl.when(qi > 0)
        def _():
            s_ref[qi - 1] += aj_ref[0]

        m = jnp.max(s_ref[0], axis=-1, keepdims=True)
        for j in range(1, nk):
            m = jnp.maximum(m, jnp.max(s_ref[j], axis=-1, keepdims=True))
        p = [jnp.exp(s_ref[j] - m) for j in range(nk)]
        l = sum(jnp.sum(x, axis=-1, keepdims=True) for x in p)
        acc = _dot(jnp.concatenate([x.astype(BF16) for x in p], axis=1), v_ref[0:tk, :])
        o_ref[...] = (acc / l).astype(o_ref.dtype)

    half = nblk // 2
    if half >= 1 and nblk % 2 == 0:
        pl.when(qi < half)(lambda: attend_blocks(half))
        pl.when(qi >= half)(lambda: attend_blocks(nblk))
    else:
        attend_blocks(nblk)


def moba_prompt(q, k, v, rel_bias, seq_len):
    m, d = q.shape
    dh = ATTN_HEAD_DIM
    nh = d // dh
    b = m // seq_len
    blk = MOBA_BLOCK
    nq = seq_len // blk
    assert seq_len % blk == 0 and nq <= _GATE_ROWS // 2
    dg, aj = prompt_bias(rel_bias)
    kv_spec = pl.BlockSpec((seq_len, dh), lambda bi, hi, qi: (bi, hi))
    tile = pl.BlockSpec((1, blk, blk), lambda bi, hi, qi: (hi, 0, 0))
    q_spec = pl.BlockSpec((blk, dh), lambda bi, hi, qi: (bi * nq + qi, hi))
    return pl.pallas_call(
        _moba_prompt_kernel,
        grid=(b, nh, nq),
        in_specs=[pl.BlockSpec(memory_space=pltpu.SMEM), q_spec, kv_spec, kv_spec, tile, tile],
        out_specs=q_spec,
        out_shape=jax.ShapeDtypeStruct((m, d), BF16),
        scratch_shapes=[pltpu.VMEM((nq, blk, blk), F32), pltpu.VMEM((_GATE_ROWS, dh), F32)],
        compiler_params=_params("parallel", "parallel", "arbitrary"),
        name="moba_prompt",
    )(rel_bias[REL_BUCKETS - 1], q, k, v, dg, aj)


def _page_specs(n, page_shape):
    zeros = (0,) * len(page_shape)
    return [pl.BlockSpec((1,) + page_shape, lambda bi, p, pt, i=i: (pt[bi, n * p + i],) + zeros) for i in range(n)]


def _ksum_kernel(pt_ref, *refs, pages_per_block):
    k_refs, o_ref = refs[:-1], refs[-1]
    for blk in range(len(k_refs) // pages_per_block):
        s = jnp.sum(k_refs[blk * pages_per_block][0], axis=0)
        for i in range(1, pages_per_block):
            s = s + jnp.sum(k_refs[blk * pages_per_block + i][0], axis=0)
        o_ref[0, blk] = s


def cache_block_ksum(cache_k, page_table):
    b, n_pages = page_table.shape
    _, page, nh, dh = cache_k.shape
    ppb = MOBA_BLOCK // page
    pps = _pick(n_pages, (4 * ppb, 2 * ppb, ppb))
    grid_spec = pltpu.PrefetchScalarGridSpec(
        num_scalar_prefetch=1,
        grid=(b, n_pages // pps),
        in_specs=_page_specs(pps, (page, nh, dh)),
        out_specs=pl.BlockSpec((1, pps // ppb, nh, dh), lambda bi, p, pt: (bi, p, 0, 0)),
    )
    return pl.pallas_call(
        functools.partial(_ksum_kernel, pages_per_block=ppb),
        grid_spec=grid_spec,
        out_shape=jax.ShapeDtypeStruct((b, n_pages // ppb, nh, dh), F32),
        compiler_params=_params("parallel", "arbitrary"),
        name="cache_block_ksum",
    )(page_table, *([cache_k] * pps))


def _sample_select_kernel(q_ref, ks_ref, sel_ref):
    nh = q_ref.shape[1]
    nblk = ks_ref.shape[2]
    gate = jnp.concatenate(
        [_dot_nt(ks_ref[0, h], q_ref[0, h].astype(F32), _HI) for h in range(nh)], axis=1)
    row = lax.broadcasted_iota(jnp.int32, gate.shape, 0)
    cnt = jnp.zeros(gate.shape, jnp.int32)
    for n in range(nblk):
        g_n = gate[n:n + 1, :]
        cnt = cnt + jnp.where((g_n > gate) | ((g_n == gate) & (n < row)), 1, 0)
    sel_ref[0] = jnp.where(cnt < MOBA_TOPK, 0.0, NEG_INF)


def sample_select(q_hq, ksum_h):
    b, nh, tq, dh = q_hq.shape
    nblk = ksum_h.shape[2]
    return pl.pallas_call(
        _sample_select_kernel,
        grid=(b,),
        in_specs=[pl.BlockSpec((1, nh, tq, dh), lambda i: (i, 0, 0, 0)),
                  pl.BlockSpec((1, nh, nblk, dh), lambda i: (i, 0, 0, 0))],
        out_specs=pl.BlockSpec((1, nblk, nh * tq), lambda i: (i, 0, 0)),
        out_shape=jax.ShapeDtypeStruct((b, nblk, nh * tq), F32),
        compiler_params=_params("parallel"),
        name="sample_select",
    )(q_hq, ksum_h)


def _sample_bias_kernel(rt_ref, tiles_ref, new_ref, *, n_heads, tq, page):
    table = lambda b: rt_ref[:, b:b + 1]
    ri = lax.broadcasted_iota(jnp.int32, tiles_ref.shape[1:], 0)
    ci = lax.broadcasted_iota(jnp.int32, tiles_ref.shape[1:], 1)
    same = ci % n_heads == ri // tq
    tiles_ref[0] = jnp.where(same, table(REL_BUCKETS - 1), NEG_INF)
    tiles_ref[1] = jnp.where(same, _rel_bias_of_dist(page + ri % tq - ci // n_heads, table), NEG_INF)
    ri = lax.broadcasted_iota(jnp.int32, new_ref.shape, 0)
    ci = lax.broadcasted_iota(jnp.int32, new_ref.shape, 1)
    d = ri % tq - ci // n_heads
    ok = (d >= 0) & (ci % n_heads == ri // tq)
    new_ref[...] = jnp.where(ok, _rel_bias_of_dist(d, table), NEG_INF)


def sample_bias(rel_bias, tq, page):
    nh = rel_bias.shape[1]
    rows = nh * tq
    rt = jnp.repeat(rel_bias.T, tq, axis=0)
    return pl.pallas_call(
        functools.partial(_sample_bias_kernel, n_heads=nh, tq=tq, page=page),
        out_shape=[jax.ShapeDtypeStruct((2, rows, page * nh), F32), jax.ShapeDtypeStruct((rows, rows), F32)],
        compiler_params=pltpu.CompilerParams(vmem_limit_bytes=V7X_VMEM_LIMIT),
        name="sample_bias",
    )(rt)


def _moba_sample_kernel(pt_ref, q_ref, kn_ref, vn_ref, *rest, ppb):
    k_refs, v_refs, tile_refs = rest[:ppb], rest[ppb:2 * ppb], rest[2 * ppb:3 * ppb]
    sel_ref, new_ref, o_ref, m_ref, l_ref, acc_ref = rest[3 * ppb:]
    p = pl.program_id(1)
    scale = ATTN_HEAD_DIM ** -0.5
    q = q_ref[0]
    rows = q.shape[0]

    @pl.when(p == 0)
    def _():
        lm = _dot_nt(q, kn_ref[0]) * scale + new_ref[...]
        m = jnp.max(lm, axis=-1, keepdims=True)
        pr = jnp.exp(lm - m)
        m_ref[...] = m
        l_ref[...] = jnp.sum(pr, axis=-1, keepdims=True)
        acc_ref[...] = _dot(pr.astype(BF16), vn_ref[0])

    ri = lax.broadcasted_iota(jnp.int32, (rows, rows), 0)
    ci = lax.broadcasted_iota(jnp.int32, (rows, rows), 1)
    picked = jnp.sum(jnp.where(ri == ci, sel_ref[0, 0], 0.0), axis=-1, keepdims=True)
    lm = [_dot_nt(q, k_refs[i][0].astype(BF16)) * scale + tile_refs[i][0] + picked for i in range(ppb)]
    m_old = m_ref[...]
    m_new = m_old
    for x in lm:
        m_new = jnp.maximum(m_new, jnp.max(x, axis=-1, keepdims=True))
    alpha = jnp.exp(m_old - m_new)
    pr = [jnp.exp(x - m_new) for x in lm]
    m_ref[...] = m_new
    l_ref[...] = alpha * l_ref[...] + sum(jnp.sum(x, axis=-1, keepdims=True) for x in pr)
    acc_ref[...] = alpha * acc_ref[...] + sum(
        _dot(pr[i].astype(BF16), v_refs[i][0].astype(BF16)) for i in range(ppb))

    @pl.when(p == pl.num_programs(1) - 1)
    def _():
        o_ref[0] = acc_ref[...] / l_ref[...]


def moba_sample(q, k_new, v_new, cache_k, cache_v, page_table, rel_bias, tq):
    m, d = q.shape
    b = m // tq
    n_pages = page_table.shape[1]
    _, page, nh, dh = cache_k.shape
    ppb = MOBA_BLOCK // page
    rows = nh * tq
    past = n_pages * page
    assert MOBA_BLOCK % page == 0 and past % MOBA_BLOCK == 0 and tq <= MOBA_BLOCK and page + 1 >= T5_SATURATION

    q_hq = q.reshape(b, tq, nh, dh).transpose(0, 2, 1, 3)
    ksum = cache_block_ksum(cache_k, page_table).transpose(0, 2, 1, 3)
    nblk = past // MOBA_BLOCK
    sel = sample_select(q_hq, ksum).reshape(b, nblk, 1, rows)
    bias_tiles, bias_new = sample_bias(rel_bias, tq, page)
    q_rows = q_hq.reshape(b, rows, dh)
    kn = k_new.astype(BF16).reshape(b, rows, dh)
    vn = v_new.astype(BF16).reshape(b, rows, dh)

    cols = page * nh
    n_phys = cache_k.shape[0]
    ck = cache_k.reshape(n_phys, cols, dh)
    cv = cache_v.reshape(n_phys, cols, dh)
    small = pl.BlockSpec((1, rows, dh), lambda bi, p, pt: (bi, 0, 0))
    pages = _page_specs(ppb, (cols, dh))
    far_tile = pl.BlockSpec((1, rows, cols), lambda bi, p, pt: (0, 0, 0))
    end_tile = pl.BlockSpec((1, rows, cols), lambda bi, p, pt: (jnp.where(p == nblk - 1, 1, 0), 0, 0))
    grid_spec = pltpu.PrefetchScalarGridSpec(
        num_scalar_prefetch=1,
        grid=(b, nblk),
        in_specs=[small, small, small] + pages + pages + [far_tile] * (ppb - 1) + [end_tile]
        + [pl.BlockSpec((1, 1, 1, rows), lambda bi, p, pt: (bi, p, 0, 0)),
           pl.BlockSpec((rows, rows), lambda bi, p, pt: (0, 0))],
        out_specs=small,
        scratch_shapes=[pltpu.VMEM((rows, 1), F32), pltpu.VMEM((rows, 1), F32), pltpu.VMEM((rows, dh), F32)],
    )
    o = pl.pallas_call(
        functools.partial(_moba_sample_kernel, ppb=ppb),
        grid_spec=grid_spec,
        out_shape=jax.ShapeDtypeStruct((b, rows, dh), F32),
        compiler_params=_params("parallel", "arbitrary"),
        name="moba_sample",
    )(page_table, q_rows, kn, vn, *([ck] * ppb), *([cv] * ppb), *([bias_tiles] * ppb), sel, bias_new)
    return o.reshape(b, nh, tq, dh).transpose(0, 2, 1, 3).reshape(m, d).astype(BF16)


def rwkv_time_mix(x, seq_len, shift0, wkv0, gain, w):
    m, d = x.shape
    (xr, xw, xk, xv, xa, xg), shift_new = rwkv_mix(x, gain, w["mu"], shift0, seq_len)
    r = matmul(xr, w["w_r"])
    k = matmul(xk, w["w_k"])
    v = matmul(xv, w["w_v"])
    wl = lora(xw, w["w1"], w["w2"], "tanh")
    al = lora(xa, w["a1"], w["a2"], "none")
    g = lora(xg, w["g1"], w["g2"], "sigmoid")
    y, h_fin = wkv(r, k, v, wl, al, g, w["prm"], jnp.swapaxes(wkv0, -1, -2), seq_len)
    x_new = matmul(y, w["w_o"], res=x)
    return x_new, jnp.swapaxes(h_fin, -1, -2), shift_new


def _rwkv_weights(i, rwkv_mu, rwkv_w_r, rwkv_w_k, rwkv_w_v, rwkv_w_o, rwkv_w0, rwkv_w1, rwkv_w2, rwkv_a0, rwkv_a1,
                  rwkv_a2, rwkv_g1, rwkv_g2, rwkv_k_k, rwkv_k_a, rwkv_r_k, rwkv_gn_g, rwkv_gn_b):
    d = rwkv_mu.shape[-1]
    w1, w2 = _pad_rank(rwkv_w1[i], rwkv_w2[i])
    a1, a2 = _pad_rank(rwkv_a1[i], rwkv_a2[i])
    g1, g2 = _pad_rank(rwkv_g1[i], rwkv_g2[i])
    prm = jnp.stack([rwkv_w0[i], rwkv_a0[i], rwkv_k_k[i], rwkv_k_a[i], rwkv_r_k[i].reshape(d), rwkv_gn_g[i],
                     rwkv_gn_b[i], jnp.zeros((d,), F32)]).astype(F32)
    return dict(mu=rwkv_mu[i], w_r=rwkv_w_r[i].astype(BF16), w_k=rwkv_w_k[i].astype(BF16),
                w_v=rwkv_w_v[i].astype(BF16), w_o=rwkv_w_o[i].astype(BF16),
                w1=w1, w2=w2, a1=a1, a2=a2, g1=g1, g2=g2, prm=prm)


def kernel(x_prompt, x_sample, state_wkv, state_shift, cache_k, cache_v, page_table, norm_mix, norm_ffn, norm_kv, norm_final, rwkv_mu, rwkv_w_r, rwkv_w_k, rwkv_w_v, rwkv_w_o, rwkv_w0, rwkv_w1, rwkv_w2, rwkv_a0, rwkv_a1, rwkv_a2, rwkv_g1, rwkv_g2, rwkv_k_k, rwkv_k_a, rwkv_r_k, rwkv_gn_g, rwkv_gn_b, kv_w, attn_w_q, attn_w_o, rel_bias, ffn_w_gate, ffn_w_up, ffn_w_down, moe_w_router, moe_b_router, moe_w_gate, moe_w_up, moe_w_down):
    d = x_prompt.shape[-1]
    nh = d // ATTN_HEAD_DIM
    assert norm_mix.shape[0] == 2 and rwkv_mu.shape[0] == 1 and attn_w_q.shape[0] == 1
    assert ffn_w_gate.shape[0] == 1 and moe_w_gate.shape[0] == 1

    w_rwkv = _rwkv_weights(0, rwkv_mu, rwkv_w_r, rwkv_w_k, rwkv_w_v, rwkv_w_o, rwkv_w0, rwkv_w1, rwkv_w2, rwkv_a0,
                           rwkv_a1, rwkv_a2, rwkv_g1, rwkv_g2, rwkv_k_k, rwkv_k_a, rwkv_r_k, rwkv_gn_g, rwkv_gn_b)
    wg, wu, wd = ffn_w_gate[0].astype(BF16), ffn_w_up[0].astype(BF16), ffn_w_down[0].astype(BF16)
    kvw_k, kvw_v = kv_w[:, :d].astype(BF16), kv_w[:, d:].astype(BF16)
    wq, wo = attn_w_q[0].astype(BF16), attn_w_o[0].astype(BF16)
    eg, eu, ed = moe_w_gate[0].astype(BF16), moe_w_up[0].astype(BF16), moe_w_down[0].astype(BF16)
    gains_kvq = jnp.stack([norm_kv, norm_mix[1]])

    def run_group(x3, wkv0, shift0, attend):
        b, t, _ = x3.shape
        x = x3.reshape(b * t, d)
        x, wkv_new, shift_new = rwkv_time_mix(x, t, shift0, wkv0, norm_mix[0], w_rwkv)
        (h,) = rmsnorm_multi(x, norm_ffn[0:1], [BF16])
        x = ffn(h, wg, wu, wd, x)
        h_kv, h_q = rmsnorm_multi(x, gains_kvq, [BF16, BF16])
        k = matmul(h_kv, kvw_k)
        v = matmul(h_kv, kvw_v)
        q = matmul(h_q, wq, out_dtype=BF16)
        o = attend(q, k, v, t)
        x = matmul(o, wo, res=x)
        y = moe_layer(x, norm_ffn[1], norm_final, moe_w_router[0], moe_b_router[0], eg, eu, ed)
        kv_shape = (b, t, nh, ATTN_HEAD_DIM)
        return y.reshape(b, t, d), wkv_new[None], shift_new[None], k.reshape(kv_shape), v.reshape(kv_shape)

    bp = x_prompt.shape[0]
    wkv_zero = jnp.zeros((bp,) + state_wkv.shape[2:], state_wkv.dtype)
    shift_zero = jnp.zeros((bp, d), state_shift.dtype)
    out_p = run_group(
        x_prompt, wkv_zero, shift_zero,
        lambda q, k, v, t: moba_prompt(q, k.astype(BF16), v.astype(BF16), rel_bias, t))
    out_s = run_group(
        x_sample, state_wkv[0], state_shift[0],
        lambda q, k, v, t: moba_sample(q, k, v, cache_k, cache_v, page_table, rel_bias, t))
    return (out_p[0], out_s[0]) + out_p[1:] + out_s[1:]
```

```python
import functools
import math

import jax
import jax.numpy as jnp
from jax import lax
from jax.experimental import pallas as pl
from jax.experimental.pallas import tpu as pltpu

F32 = jnp.float32
BF16 = jnp.bfloat16

RWKV_HEAD = 64
ATTN_HEAD_DIM = 128
MOBA_BLOCK = 256
MOBA_TOPK = 3
REL_BUCKETS = 32
REL_MAX_DIST = 128
MOE_TOPK = 2
GN_EPS = 64e-5
RMS_EPS = 1e-6
NEG_INF = -1e30

V7X_VMEM_LIMIT = 48 * 1024 * 1024
LANES = 128

_HI = lax.Precision.HIGHEST


def _params(*sem):
    return pltpu.CompilerParams(dimension_semantics=sem, vmem_limit_bytes=V7X_VMEM_LIMIT)


def _dot(a, b):
    return jnp.dot(a, b, preferred_element_type=F32)


def _dot_hi(a, b):
    return jnp.dot(a, b, preferred_element_type=F32, precision=_HI)


def _dot_nt(a, b, precision=None):
    return lax.dot_general(a, b, (((1,), (1,)), ((), ())), preferred_element_type=F32, precision=precision)


def _pick(n, prefs):
    for p in prefs:
        if n % p == 0:
            return p
    return n


def _mm_kernel(x_ref, w_ref, *rest, has_res):
    res_ref = rest[0] if has_res else None
    o_refs = rest[1:] if has_res else rest
    acc = _dot(x_ref[...], w_ref[...])
    if has_res:
        acc = acc + res_ref[...]
    for o_ref in o_refs:
        o_ref[...] = acc.astype(o_ref.dtype)


def matmul(x, w, res=None, out_dtypes=(F32,)):
    m, k = x.shape
    n = w.shape[1]
    tm = _pick(m, (1024, 512, 256, 128, 64, 32, 16, 8))
    tn = _pick(n, (512, 256, 128))
    in_specs = [pl.BlockSpec((tm, k), lambda i, j: (i, 0)), pl.BlockSpec((k, tn), lambda i, j: (0, j))]
    args = [x, w]
    if res is not None:
        in_specs.append(pl.BlockSpec((tm, tn), lambda i, j: (i, j)))
        args.append(res)
    outs = pl.pallas_call(
        functools.partial(_mm_kernel, has_res=res is not None),
        grid=(m // tm, n // tn),
        in_specs=in_specs,
        out_specs=[pl.BlockSpec((tm, tn), lambda i, j: (i, j)) for _ in out_dtypes],
        out_shape=[jax.ShapeDtypeStruct((m, n), dt) for dt in out_dtypes],
        compiler_params=_params("parallel", "arbitrary"),
        name="matmul",
    )(*args)
    return outs[0] if len(out_dtypes) == 1 else outs


def _rms(xf):
    return xf * lax.rsqrt(jnp.mean(xf * xf, axis=-1, keepdims=True) + RMS_EPS)


def _norm_kernel(x_ref, g_ref, *o_refs):
    y = _rms(x_ref[...])
    for j, o_ref in enumerate(o_refs):
        o_ref[...] = (y * g_ref[j:j + 1, :]).astype(o_ref.dtype)


def rmsnorm_multi(x, gains, out_dtypes):
    m, d = x.shape
    n = gains.shape[0]
    tm = _pick(m, (512, 256, 128, 64, 32, 16, 8))
    return pl.pallas_call(
        _norm_kernel,
        grid=(m // tm,),
        in_specs=[pl.BlockSpec((tm, d), lambda i: (i, 0)), pl.BlockSpec((n, d), lambda i: (0, 0))],
        out_specs=[pl.BlockSpec((tm, d), lambda i: (i, 0)) for _ in range(n)],
        out_shape=[jax.ShapeDtypeStruct((m, d), dt) for dt in out_dtypes],
        compiler_params=_params("parallel"),
        name="rmsnorm",
    )(x, gains)


def _mix_kernel(x_ref, g_ref, mu_ref, s0_ref, *rest, blocks_per_seq):
    o_refs, last_ref, carry_ref = rest[:6], rest[6], rest[7]
    i = pl.program_id(0)
    h = _rms(x_ref[...]) * g_ref[...]
    tm = h.shape[0]

    @pl.when(i % blocks_per_seq == 0)
    def _():
        carry_ref[...] = s0_ref[0]

    row = lax.broadcasted_iota(jnp.int32, h.shape, 0)
    prev = jnp.where(row == 0, carry_ref[...], pltpu.roll(h, 1, 0))
    xx = prev - h
    for j in range(6):
        o_refs[j][...] = (h + xx * mu_ref[j:j + 1, :]).astype(o_refs[j].dtype)
    last = h[tm - 1:tm, :]
    carry_ref[...] = last
    last_ref[0] = last


def rwkv_mix(x, gain, mu, shift0, seq_len):
    m, d = x.shape
    b = m // seq_len
    tm = _pick(seq_len, (256, 128, 64, 32, 16, 8))
    bps = seq_len // tm
    outs = pl.pallas_call(
        functools.partial(_mix_kernel, blocks_per_seq=bps),
        grid=(m // tm,),
        in_specs=[
            pl.BlockSpec((tm, d), lambda i: (i, 0)),
            pl.BlockSpec((1, d), lambda i: (0, 0)),
            pl.BlockSpec((6, d), lambda i: (0, 0)),
            pl.BlockSpec((1, 1, d), lambda i: (i // bps, 0, 0)),
        ],
        out_specs=[pl.BlockSpec((tm, d), lambda i: (i, 0)) for _ in range(6)]
        + [pl.BlockSpec((1, 1, d), lambda i: (i // bps, 0, 0))],
        out_shape=[jax.ShapeDtypeStruct((m, d), BF16) for _ in range(6)]
        + [jax.ShapeDtypeStruct((b, 1, d), F32)],
        scratch_shapes=[pltpu.VMEM((1, d), F32)],
        compiler_params=_params("arbitrary"),
        name="rwkv_mix",
    )(x, gain.reshape(1, d), mu, shift0.reshape(b, 1, d))
    return outs[:6], outs[6].reshape(b, d)


def _lora_kernel(x_ref, w1_ref, w2_ref, o_ref, *, act):
    t = _dot(x_ref[...], w1_ref[...])
    if act == "tanh":
        t = jnp.tanh(t)
    elif act == "sigmoid":
        t = 1.0 / (1.0 + jnp.exp(-t))
    o_ref[...] = _dot(t.astype(BF16), w2_ref[...])


def lora(x, w1, w2, act):
    m, d = x.shape
    r = w1.shape[1]
    n = w2.shape[1]
    tm = _pick(m, (512, 256, 128, 64, 32, 16, 8))
    return pl.pallas_call(
        functools.partial(_lora_kernel, act=act),
        grid=(m // tm,),
        in_specs=[
            pl.BlockSpec((tm, d), lambda i: (i, 0)),
            pl.BlockSpec((d, r), lambda i: (0, 0)),
            pl.BlockSpec((r, n), lambda i: (0, 0)),
        ],
        out_specs=pl.BlockSpec((tm, n), lambda i: (i, 0)),
        out_shape=jax.ShapeDtypeStruct((m, n), F32),
        compiler_params=_params("parallel"),
        name="lora",
    )(x, w1, w2)


def _pad_rank(w1, w2):
    r = w1.shape[1]
    rp = -(-r // LANES) * LANES
    if rp != r:
        w1 = jnp.pad(w1, ((0, 0), (0, rp - r)))
        w2 = jnp.pad(w2, ((0, rp - r), (0, 0)))
    return w1.astype(BF16), w2.astype(BF16)


def _split(a):
    hi = a.astype(BF16)
    return hi, (a - hi.astype(F32)).astype(BF16)


def _dot3(a, b, mode):
    (ah, al), (bh, bl) = a, b
    la = 0 if mode == "tn" else 1
    lb = 1 if mode == "nt" else 0
    lhs = jnp.concatenate([ah, ah, al], axis=la)
    rhs = jnp.concatenate([bh, bl, bh], axis=lb)
    return lax.dot_general(lhs, rhs, (((la,), (lb,)), ((), ())), preferred_element_type=F32)


def _wkv_kernel(r_ref, k_ref, v_ref, wl_ref, al_ref, g_ref, prm_ref, h0_ref, y_ref, hout_ref, h_scr,
                *, chunk, pairs):
    c_idx = pl.program_id(2)
    n = RWKV_HEAD
    pw = 2 * n
    cw = 2 * chunk
    tb = r_ref.shape[0]
    n_chunks = tb // chunk

    def iota(shape, dim):
        return lax.broadcasted_iota(jnp.int32, shape, dim)

    def block_diag(x, w):
        even = iota((1, 2 * w), 1) < w
        return jnp.concatenate([jnp.where(even, x, 0.0), jnp.where(even, 0.0, x)], axis=0)

    def split_bd(x, w):
        even = jnp.where(iota((1, 2 * w), 1) < w, 1.0, 0.0).astype(BF16)
        hi, lo = _split(x)
        return tuple(jnp.concatenate([t * even, t * (1 - even)], axis=0) for t in (hi, lo))

    def cat_rows(a, b):
        return tuple(jnp.concatenate([s, t], axis=0) for s, t in zip(a, b))

    @pl.when(c_idx == 0)
    def _():
        for p in range(pairs):
            h_scr[p] = block_diag(jnp.concatenate([h0_ref[0, 2 * p], h0_ref[0, 2 * p + 1]], axis=1), n)

    ri = iota((chunk, cw), 0)
    ci = iota((chunk, cw), 1) % chunk
    strict = ri > ci
    incl = ri >= ci
    eye_c = jnp.where(ri == ci, 1.0, 0.0)
    tri = jnp.where(iota((chunk, chunk), 0) >= iota((chunk, chunk), 1), 1.0, 0.0).astype(BF16)
    tri3 = jnp.concatenate([tri, tri, tri], axis=1)
    same_head = iota((pw, pw), 0) // n == iota((pw, pw), 1) // n
    seg = jnp.where(same_head, 1.0, 0.0).astype(BF16)
    seg2 = jnp.concatenate([seg, seg], axis=0)
    eye_pw = jnp.where(iota((pw, pw), 0) == iota((pw, pw), 1), 1.0, 0.0)
    n_double = max(int(math.log2(chunk)) - 1, 0)

    def head_sum(x):
        x_h, x_l = _split(x)
        return _dot(jnp.concatenate([x_h, x_l], axis=1), seg2)

    def one_chunk(ci_, hs):
        rows = pl.ds(pl.multiple_of(ci_ * chunk, chunk), chunk)
        r_all = r_ref[rows, :]
        k_all = k_ref[rows, :]
        v_all = v_ref[rows, :]
        wl_all = wl_ref[rows, :]
        al_all = al_ref[rows, :]
        pr = range(pairs)
        sls = [slice(j * pw, (j + 1) * pw) for j in pr]
        prm = lambda row, j: prm_ref[row:row + 1, sls[j]]
        r = [r_all[:, sl] for sl in sls]
        k = [k_all[:, sl] for sl in sls]
        v = [v_all[:, sl] for sl in sls]
        z = [-(prm(0, j) + wl_all[:, sls[j]]) for j in pr]
        ld = [-jnp.exp(-(jnp.maximum(zj, 0.0) + jnp.log(1.0 + jnp.exp(-jnp.abs(zj)))) - 0.5) for zj in z]
        alpha = [1.0 / (1.0 + jnp.exp(-(prm(1, j) + al_all[:, sls[j]]))) for j in pr]
        kkv = [k[j] * prm(2, j) for j in pr]
        kk = [kkv[j] / jnp.maximum(jnp.sqrt(head_sum(kkv[j] * kkv[j])), 1e-12) for j in pr]
        k2 = [k[j] * (1.0 + (alpha[j] - 1.0) * prm(3, j)) for j in pr]

        def ld3(x):
            x_h, x_m = _split(x)
            return jnp.concatenate([x_h, x_m, (x - x_h.astype(F32) - x_m.astype(F32)).astype(BF16)], axis=0)

        cum = [_dot(tri3, ld3(ld[j])) for j in pr]
        w_in = [jnp.exp(c) for c in cum]
        w_inv = [jnp.exp(-c) for c in cum]
        w_tot = [w[chunk - 1:chunk, :] for w in w_in]
        ar = [_split(jnp.concatenate([-kk[j] * jnp.exp(cum[j] - ld[j]), r[j] * w_in[j]], axis=0)) for j in pr]
        b_t = [kk[j] * alpha[j] * w_inv[j] for j in pr]
        k_t = [k2[j] * w_inv[j] for j in pr]
        bk_bd = [cat_rows(split_bd(b_t[j], n), split_bd(k_t[j], n)) for j in pr]
        big = [_dot3(ar[j], bk_bd[j], "nt") for j in pr]
        a_ab = [jnp.where(strict, b_[:chunk, :cw], 0.0) for b_ in big]
        a_ak = [jnp.where(strict, b_[:chunk, cw:], 0.0) for b_ in big]
        a_r = [jnp.concatenate([jnp.where(incl, b_[chunk:, :cw], 0.0),
                                jnp.where(incl, b_[chunk:, cw:], 0.0)], axis=1) for b_ in big]
        tinv = [eye_c + a for a in a_ab]
        p = a_ab
        for _ in range(n_double):
            p = [_dot3(_split(x), split_bd(x, chunk), "nn") for x in p]
            tinv = [tinv[j] + _dot3(_split(tinv[j]), split_bd(p[j], chunk), "nn") for j in pr]

        hdep = [_dot3(ar[j], _split(hs[j]), "nn") for j in pr]
        v_bd = [split_bd(v[j], n) for j in pr]
        akv = [_dot3(_split(a_ak[j]), v_bd[j], "nn") for j in pr]
        u = [_dot3(_split(tinv[j]), split_bd(hdep[j][:chunk] + akv[j], n), "nn") for j in pr]
        uv_bd = [cat_rows(split_bd(u[j], n), v_bd[j]) for j in pr]
        o = [hdep[j][chunk:] + _dot3(_split(a_r[j]), uv_bd[j], "nn") for j in pr]
        w_col = [jnp.sum(eye_pw * w_tot[j], axis=1, keepdims=True) for j in pr]
        bk_s = [_split(jnp.concatenate([b_t[j], k_t[j]], axis=0) * w_tot[j]) for j in pr]
        uv = [_split(jnp.concatenate([u[j], v[j]], axis=0)) for j in pr]
        h_new = [hs[j] * w_col[j] + jnp.where(same_head, _dot3(bk_s[j], uv[j], "tn"), 0.0) for j in pr]

        outs = []
        for j in pr:
            d_ = o[j] - head_sum(o[j]) * (1.0 / n)
            var = head_sum(d_ * d_) * (1.0 / n)
            on = d_ * lax.rsqrt(var + GN_EPS) * prm(5, j) + prm(6, j)
            outs.append(on + head_sum(r[j] * k2[j] * prm(4, j)) * v[j])
        y_ref[rows, :] = (jnp.concatenate(outs, axis=1) * g_ref[rows, :]).astype(y_ref.dtype)
        return tuple(h_new)

    hs = lax.fori_loop(0, n_chunks, one_chunk, tuple(h_scr[j] for j in range(pairs)))
    for j in range(pairs):
        h_scr[j] = hs[j]

    @pl.when(c_idx == pl.num_programs(2) - 1)
    def _():
        for j in range(pairs):
            hout_ref[0, 2 * j] = hs[j][:n, :n]
            hout_ref[0, 2 * j + 1] = hs[j][n:, n:]


def wkv(r, k, v, wl, al, g, prm, h0, seq_len):
    m, d = r.shape
    b = m // seq_len
    n = RWKV_HEAD
    nh = d // n
    heads = _pick(nh, (16, 8, 4, 2))
    assert nh % heads == 0 and heads % 2 == 0
    chunk = min(64, seq_len)
    tb = _pick(seq_len, (512, 256, 128, 64, 32, 16, 8))
    nb = seq_len // tb
    wd = heads * n
    row_spec = pl.BlockSpec((tb, wd), lambda bi, hi, ci: (bi * nb + ci, hi))
    st_spec = pl.BlockSpec((1, heads, n, n), lambda bi, hi, ci: (bi, hi, 0, 0))
    return pl.pallas_call(
        functools.partial(_wkv_kernel, chunk=chunk, pairs=heads // 2),
        grid=(b, nh // heads, nb),
        in_specs=[row_spec] * 6 + [pl.BlockSpec((8, wd), lambda bi, hi, ci: (0, hi)), st_spec],
        out_specs=[row_spec, st_spec],
        out_shape=[jax.ShapeDtypeStruct((m, d), BF16), jax.ShapeDtypeStruct((b, nh, n, n), F32)],
        scratch_shapes=[pltpu.VMEM((heads // 2, 2 * n, 2 * n), F32)],
        compiler_params=_params("parallel", "parallel", "arbitrary"),
        name="wkv",
    )(r, k, v, wl, al, g, prm, h0)


def _silu_mul(g, u):
    return g * (1.0 / (1.0 + jnp.exp(-g))) * u


def _ffn_kernel(x_ref, g_ref, wg_ref, wu_ref, wd_ref, o_ref, h_ref, acc_ref):
    j = pl.program_id(1)

    @pl.when(j == 0)
    def _():
        x = x_ref[...]
        acc_ref[...] = x
        h_ref[...] = (_rms(x) * g_ref[...]).astype(BF16)

    h = h_ref[...]
    a = _silu_mul(_dot(h, wg_ref[...]), _dot(h, wu_ref[...])).astype(BF16)
    acc_ref[...] += _dot(a, wd_ref[...])

    @pl.when(j == pl.num_programs(1) - 1)
    def _():
        o_ref[...] = acc_ref[...]


def ffn(x, gain, wg, wu, wd):
    m, d = x.shape
    f = wg.shape[1]
    tm = _pick(m, (512, 256, 128, 64, 32, 16, 8))
    tf = _pick(f, (512, 256, 128))
    return pl.pallas_call(
        _ffn_kernel,
        grid=(m // tm, f // tf),
        in_specs=[
            pl.BlockSpec((tm, d), lambda i, j: (i, 0)),
            pl.BlockSpec((1, d), lambda i, j: (0, 0)),
            pl.BlockSpec((d, tf), lambda i, j: (0, j)),
            pl.BlockSpec((d, tf), lambda i, j: (0, j)),
            pl.BlockSpec((tf, d), lambda i, j: (j, 0)),
        ],
        out_specs=pl.BlockSpec((tm, d), lambda i, j: (i, 0)),
        out_shape=jax.ShapeDtypeStruct((m, d), F32),
        scratch_shapes=[pltpu.VMEM((tm, d), BF16), pltpu.VMEM((tm, d), F32)],
        compiler_params=_params("parallel", "arbitrary"),
        name="ffn",
    )(x, gain.reshape(1, d), wg, wu, wd)


def _moe_ffn_kernel(te_ref, tv_ref, src_ref, x_hbm, wg_ref, wu_ref, wd_ref, o_ref, xbuf, acc_ref, sems,
                    *, issue_steps):
    i = pl.program_id(0)
    j = pl.program_id(1)
    tm = o_ref.shape[0]
    per_step = tm // issue_steps
    slot = i % 2

    def wait_tile(s):
        def wait(r, c):
            _row_copy(x_hbm, xbuf.at[s], sems.at[s], 0, r).wait()
            return c
        lax.fori_loop(0, tm, wait, 0)

    @pl.when((i == 0) & (j == 0))
    def _():
        def start(r, c):
            _row_copy(x_hbm, xbuf.at[0], sems.at[0], src_ref[r], r).start()
            return c
        lax.fori_loop(0, tm, start, 0)

    @pl.when((i + 1 < pl.num_programs(0)) & (j < issue_steps))
    def _():
        base = j * per_step
        for r in range(per_step):
            _row_copy(x_hbm, xbuf.at[1 - slot], sems.at[1 - slot], src_ref[(i + 1) * tm + base + r], base + r).start()

    @pl.when(j == 0)
    def _():
        wait_tile(slot)

    @pl.when(tv_ref[i] > 0)
    def _():
        @pl.when(j == 0)
        def _():
            acc_ref[...] = jnp.zeros_like(acc_ref)

        x = xbuf[slot].astype(BF16)
        a = _silu_mul(_dot(x, wg_ref[0]), _dot(x, wu_ref[0])).astype(BF16)
        acc_ref[...] += _dot(a, wd_ref[0])

        @pl.when(j == pl.num_programs(1) - 1)
        def _():
            o_ref[...] = acc_ref[...]

    @pl.when(tv_ref[i] == 0)
    def _():
        o_ref[...] = jnp.zeros_like(o_ref)


def moe_ffn(x, src_token, wg, wu, wd, tile_expert, tile_valid, tm):
    d = x.shape[1]
    mp = src_token.shape[0]
    f = wg.shape[2]
    tf = _pick(f, (512, 256, 128))
    nf = f // tf
    issue_steps = max(s for s in (8, 4, 2, 1) if s <= nf)

    def fj(i, j, tv):
        return jnp.where(tv[i] > 0, j, nf - 1)

    grid_spec = pltpu.PrefetchScalarGridSpec(
        num_scalar_prefetch=3,
        grid=(mp // tm, nf),
        in_specs=[
            pl.BlockSpec(memory_space=pl.ANY),
            pl.BlockSpec((1, d, tf), lambda i, j, te, tv, src: (te[i], 0, fj(i, j, tv))),
            pl.BlockSpec((1, d, tf), lambda i, j, te, tv, src: (te[i], 0, fj(i, j, tv))),
            pl.BlockSpec((1, tf, d), lambda i, j, te, tv, src: (te[i], fj(i, j, tv), 0)),
        ],
        out_specs=pl.BlockSpec((tm, d), lambda i, j, te, tv, src: (i, 0)),
        scratch_shapes=[pltpu.VMEM((2, tm, d), F32), pltpu.VMEM((tm, d), F32), pltpu.SemaphoreType.DMA((2,))],
    )
    return pl.pallas_call(
        functools.partial(_moe_ffn_kernel, issue_steps=issue_steps),
        grid_spec=grid_spec,
        out_shape=jax.ShapeDtypeStruct((mp, d), F32),
        compiler_params=_params("arbitrary", "arbitrary"),
        name="moe_ffn",
    )(tile_expert, tile_valid, src_token, x, wg, wu, wd)


def _router_kernel(x_ref, g_ref, wr_ref, br_ref, h_ref, rt_ref):
    h = _rms(x_ref[...]) * g_ref[...]
    h_ref[...] = h
    logits = _dot_hi(h, wr_ref[...]) + br_ref[...]
    lane = lax.broadcasted_iota(jnp.int32, logits.shape, 1)
    m1 = jnp.max(logits, axis=-1, keepdims=True)
    i1 = jnp.min(jnp.where(logits == m1, lane, LANES), axis=-1, keepdims=True)
    rest = jnp.where(lane == i1, -jnp.inf, logits)
    m2 = jnp.max(rest, axis=-1, keepdims=True)
    i2 = jnp.min(jnp.where(rest == m2, lane, LANES), axis=-1, keepdims=True)
    e = jnp.exp(m2 - m1)
    g1 = 1.0 / (1.0 + e)
    g2 = e / (1.0 + e)
    rt_ref[...] = jnp.where(lane == 0, g1, jnp.where(lane == 1, g2, jnp.where(
        lane == 2, i1.astype(F32), jnp.where(lane == 3, i2.astype(F32), 0.0))))


def router(x, gain, w_router, b_router):
    m, d = x.shape
    e = w_router.shape[1]
    wr = jnp.pad(w_router.astype(F32), ((0, 0), (0, LANES - e)))
    br = jnp.pad(b_router.astype(F32), (0, LANES - e), constant_values=NEG_INF).reshape(1, LANES)
    tm = _pick(m, (512, 256, 128, 64, 32, 16, 8))
    return pl.pallas_call(
        _router_kernel,
        grid=(m // tm,),
        in_specs=[
            pl.BlockSpec((tm, d), lambda i: (i, 0)),
            pl.BlockSpec((1, d), lambda i: (0, 0)),
            pl.BlockSpec((d, LANES), lambda i: (0, 0)),
            pl.BlockSpec((1, LANES), lambda i: (0, 0)),
        ],
        out_specs=[pl.BlockSpec((tm, d), lambda i: (i, 0)), pl.BlockSpec((tm, LANES), lambda i: (i, 0))],
        out_shape=[jax.ShapeDtypeStruct((m, d), F32), jax.ShapeDtypeStruct((m, LANES), F32)],
        compiler_params=_params("parallel"),
        name="router",
    )(x, gain.reshape(1, d), wr, br)


def _row_copy(src_hbm, dst_ref, sem, src_row, dst_row):
    return pltpu.make_async_copy(src_hbm.at[pl.ds(src_row, 1)], dst_ref.at[pl.ds(dst_row, 1)], sem)


def _combine_kernel(p1_ref, p2_ref, y_hbm, x_ref, rt_ref, g_ref, o_ref, b1_ref, b2_ref, sem1, sem2):
    tr = x_ref.shape[0]
    base = pl.program_id(0) * tr

    def start(r, c):
        _row_copy(y_hbm, b1_ref, sem1, p1_ref[base + r], r).start()
        _row_copy(y_hbm, b2_ref, sem2, p2_ref[base + r], r).start()
        return c

    def wait(r, c):
        _row_copy(y_hbm, b1_ref, sem1, 0, r).wait()
        _row_copy(y_hbm, b2_ref, sem2, 0, r).wait()
        return c

    lax.fori_loop(0, tr, start, 0)
    lax.fori_loop(0, tr, wait, 0)
    rt = rt_ref[...]
    x = x_ref[...] + rt[:, 0:1] * b1_ref[...] + rt[:, 1:2] * b2_ref[...]
    o_ref[...] = _rms(x) * g_ref[...]


def moe_combine(x, y_sorted, rt, pos1, pos2, gain):
    m, d = x.shape
    tr = _pick(m, (128, 64, 32, 16, 8))
    grid_spec = pltpu.PrefetchScalarGridSpec(
        num_scalar_prefetch=2,
        grid=(m // tr,),
        in_specs=[
            pl.BlockSpec(memory_space=pl.ANY),
            pl.BlockSpec((tr, d), lambda i, p1, p2: (i, 0)),
            pl.BlockSpec((tr, LANES), lambda i, p1, p2: (i, 0)),
            pl.BlockSpec((1, d), lambda i, p1, p2: (0, 0)),
        ],
        out_specs=pl.BlockSpec((tr, d), lambda i, p1, p2: (i, 0)),
        scratch_shapes=[pltpu.VMEM((tr, d), F32), pltpu.VMEM((tr, d), F32),
                        pltpu.SemaphoreType.DMA(()), pltpu.SemaphoreType.DMA(())],
    )
    return pl.pallas_call(
        _combine_kernel,
        grid_spec=grid_spec,
        out_shape=jax.ShapeDtypeStruct((m, d), F32),
        compiler_params=_params("arbitrary"),
        name="moe_combine",
    )(pos1, pos2, y_sorted, x, rt, gain.reshape(1, d))


def moe_layer(x, gain_ffn, gain_final, w_router, b_router, wg, wu, wd):
    m, d = x.shape
    n_exp = wg.shape[0]
    h, rt = router(x, gain_ffn, w_router, b_router)
    tm = 512 if m * MOE_TOPK >= 8 * 512 else 128
    experts = rt[:, 2:4].astype(jnp.int32).reshape(-1)
    n_asg = experts.shape[0]
    counts = jnp.sum(experts[:, None] == jnp.arange(n_exp)[None, :], axis=0)
    padded = ((counts + tm - 1) // tm) * tm
    start = jnp.cumsum(counts) - counts
    pstart = jnp.cumsum(padded) - padded
    order = jnp.argsort(experts, stable=True)
    rank = jnp.zeros((n_asg,), jnp.int32).at[order].set(jnp.arange(n_asg, dtype=jnp.int32))
    dest = (pstart[experts] + rank - start[experts]).astype(jnp.int32)
    mp = n_asg + n_exp * tm
    src_token = jnp.zeros((mp,), jnp.int32).at[dest].set(jnp.arange(n_asg, dtype=jnp.int32) // MOE_TOPK)
    tile_start = jnp.arange(mp // tm, dtype=jnp.int32) * tm
    pend = jnp.cumsum(padded)
    tile_expert = jnp.minimum(jnp.sum(tile_start[:, None] >= pend[None, :], axis=1), n_exp - 1).astype(jnp.int32)
    tile_valid = (tile_start < pend[-1]).astype(jnp.int32)
    last_used = jnp.max(jnp.where(tile_valid > 0, tile_expert, 0))
    tile_expert = jnp.where(tile_valid > 0, tile_expert, last_used)

    y_sorted = moe_ffn(h, src_token, wg, wu, wd, tile_expert, tile_valid, tm)
    pos = dest.reshape(m, MOE_TOPK)
    return moe_combine(x, y_sorted, rt, pos[:, 0], pos[:, 1], gain_final)


_T5_EXACT = REL_BUCKETS // 2
_T5_STEPS = [_T5_EXACT] + [
    math.ceil(_T5_EXACT * (REL_MAX_DIST / _T5_EXACT) ** (n / (REL_BUCKETS - _T5_EXACT)))
    for n in range(1, REL_BUCKETS - _T5_EXACT)]
T5_SATURATION = _T5_STEPS[-1]


def _rel_bias_of_dist(d, table):
    val = jnp.where(d >= 1, table(1), table(0))
    for b in range(2, _T5_EXACT):
        val = jnp.where(d >= b, table(b), val)
    for n, t in enumerate(_T5_STEPS):
        val = jnp.where(d >= t, table(_T5_EXACT + n), val)
    return val


def _prompt_bias_kernel(rb_ref, dg_ref, aj_ref):
    h = pl.program_id(0)
    blk = dg_ref.shape[1]
    ri = lax.broadcasted_iota(jnp.int32, (blk, blk), 0)
    ci = lax.broadcasted_iota(jnp.int32, (blk, blk), 1)
    table = lambda b: rb_ref[h, b]
    far = table(REL_BUCKETS - 1)
    dg_ref[0] = jnp.where(ri >= ci, _rel_bias_of_dist(ri - ci, table) - far, NEG_INF)
    aj_ref[0] = _rel_bias_of_dist(ri - ci + blk, table) - far


def prompt_bias(rel_bias):
    nh = rel_bias.shape[1]
    blk = MOBA_BLOCK
    tile = pl.BlockSpec((1, blk, blk), lambda h: (h, 0, 0))
    return pl.pallas_call(
        _prompt_bias_kernel,
        grid=(nh,),
        in_specs=[pl.BlockSpec(memory_space=pltpu.SMEM)],
        out_specs=[tile, tile],
        out_shape=[jax.ShapeDtypeStruct((nh, blk, blk), F32)] * 2,
        compiler_params=_params("arbitrary"),
        name="prompt_bias",
    )(rel_bias.T)


_GATE_ROWS = 16


def _moba_prompt_kernel(c_last_ref, q_ref, k_ref, v_ref, dg_ref, aj_ref, o_ref, s_ref, ks_ref):
    h = pl.program_id(1)
    qi = pl.program_id(2)
    blk = MOBA_BLOCK
    nblk = k_ref.shape[0] // blk
    scale = ATTN_HEAD_DIM ** -0.5
    q = q_ref[...]
    c_last = c_last_ref[h]

    @pl.when(qi == 0)
    def _():
        ks_ref[...] = jnp.zeros_like(ks_ref)
        for j in range(nblk):
            ks_ref[j:j + 1, :] = jnp.sum(k_ref[j * blk:(j + 1) * blk, :].astype(F32), axis=0, keepdims=True)

    half_rows = _GATE_ROWS // 2
    ks_hi, ks_lo = _split(ks_ref[...])
    gate = (_dot_nt(ks_hi, q) + _dot_nt(ks_lo, q))[:half_rows]
    row = lax.broadcasted_iota(jnp.int32, gate.shape, 0)
    gate = jnp.where(row < qi, gate, -jnp.inf)
    cnt = jnp.zeros(gate.shape, jnp.int32)
    for n in range(nblk):
        g_n = gate[n:n + 1, :]
        cnt = cnt + jnp.where((g_n > gate) | ((g_n == gate) & (n < row)), 1, 0)
    attend = ((cnt < MOBA_TOPK) & (row < qi)) | (row == qi)
    c_full = jnp.full(gate.shape, c_last, F32)
    c_hi = c_full.astype(BF16).astype(F32)
    mask_t = jnp.concatenate([jnp.where(attend, c_hi, NEG_INF), jnp.where(attend, c_full - c_hi, 0.0)],
                             axis=0).astype(BF16)

    def attend_blocks(nk):
        tk = nk * blk
        kb = lax.broadcasted_iota(jnp.int32, (_GATE_ROWS, tk), 1) // blk
        rb = lax.broadcasted_iota(jnp.int32, (_GATE_ROWS, tk), 0) % half_rows
        expand = jnp.where(kb == rb, 1.0, 0.0).astype(BF16)
        mask = lax.dot_general(mask_t, expand, (((0,), (0,)), ((), ())), preferred_element_type=F32)
        lm = _dot_nt(q, k_ref[0:tk, :]) * scale + mask
        for j in range(nk):
            s_ref[j] = lm[:, j * blk:(j + 1) * blk]
        s_ref[qi] += dg_ref[0]

        @pl.when(qi > 0)
        def _():
            s_ref[qi - 1] += aj_ref[0]

        m = jnp.max(s_ref[0], axis=-1, keepdims=True)
        for j in range(1, nk):
            m = jnp.maximum(m, jnp.max(s_ref[j], axis=-1, keepdims=True))
        p = [jnp.exp(s_ref[j] - m) for j in range(nk)]
        l = sum(jnp.sum(x, axis=-1, keepdims=True) for x in p)
        acc = _dot(jnp.concatenate([x.astype(BF16) for x in p], axis=1), v_ref[0:tk, :])
        o_ref[...] = (acc / l).astype(o_ref.dtype)

    lo = 0
    for nk in range(2, nblk + 2, 2):
        nk = min(nk, nblk)
        pl.when((qi >= lo) & (qi < nk))(functools.partial(attend_blocks, nk))
        lo = nk


def moba_prompt(q, k, v, rel_bias, seq_len):
    m, d = q.shape
    dh = ATTN_HEAD_DIM
    nh = d // dh
    b = m // seq_len
    blk = MOBA_BLOCK
    nq = seq_len // blk
    assert seq_len % blk == 0 and nq <= _GATE_ROWS // 2
    dg, aj = prompt_bias(rel_bias)
    kv_spec = pl.BlockSpec((seq_len, dh), lambda bi, hi, qi: (bi, hi))
    tile = pl.BlockSpec((1, blk, blk), lambda bi, hi, qi: (hi, 0, 0))
    q_spec = pl.BlockSpec((blk, dh), lambda bi, hi, qi: (bi * nq + qi, hi))
    return pl.pallas_call(
        _moba_prompt_kernel,
        grid=(b, nh, nq),
        in_specs=[pl.BlockSpec(memory_space=pltpu.SMEM), q_spec, kv_spec, kv_spec, tile, tile],
        out_specs=q_spec,
        out_shape=jax.ShapeDtypeStruct((m, d), BF16),
        scratch_shapes=[pltpu.VMEM((nq, blk, blk), F32), pltpu.VMEM((_GATE_ROWS, dh), F32)],
        compiler_params=_params("parallel", "parallel", "arbitrary"),
        name="moba_prompt",
    )(rel_bias[REL_BUCKETS - 1], q, k, v, dg, aj)


def _page_specs(n, page_shape):
    zeros = (0,) * len(page_shape)
    return [pl.BlockSpec((1,) + page_shape, lambda bi, p, pt, i=i: (pt[bi, n * p + i],) + zeros) for i in range(n)]


def _ksum_kernel(pt_ref, *refs, pages_per_block):
    k_refs, o_ref = refs[:-1], refs[-1]
    for blk in range(len(k_refs) // pages_per_block):
        s = jnp.sum(k_refs[blk * pages_per_block][0], axis=0)
        for i in range(1, pages_per_block):
            s = s + jnp.sum(k_refs[blk * pages_per_block + i][0], axis=0)
        o_ref[0, blk] = s


def cache_block_ksum(cache_k, page_table):
    b, n_pages = page_table.shape
    _, page, nh, dh = cache_k.shape
    ppb = MOBA_BLOCK // page
    pps = _pick(n_pages, (4 * ppb, 2 * ppb, ppb))
    grid_spec = pltpu.PrefetchScalarGridSpec(
        num_scalar_prefetch=1,
        grid=(b, n_pages // pps),
        in_specs=_page_specs(pps, (page, nh, dh)),
        out_specs=pl.BlockSpec((1, pps // ppb, nh, dh), lambda bi, p, pt: (bi, p, 0, 0)),
    )
    return pl.pallas_call(
        functools.partial(_ksum_kernel, pages_per_block=ppb),
        grid_spec=grid_spec,
        out_shape=jax.ShapeDtypeStruct((b, n_pages // ppb, nh, dh), F32),
        compiler_params=_params("parallel", "arbitrary"),
        name="cache_block_ksum",
    )(page_table, *([cache_k] * pps))


def _sample_select_kernel(q_ref, ks_ref, sel_ref):
    nh = q_ref.shape[1]
    nblk = ks_ref.shape[2]
    gate = jnp.concatenate(
        [_dot_nt(ks_ref[0, h], q_ref[0, h].astype(F32), _HI) for h in range(nh)], axis=1)
    row = lax.broadcasted_iota(jnp.int32, gate.shape, 0)
    cnt = jnp.zeros(gate.shape, jnp.int32)
    for n in range(nblk):
        g_n = gate[n:n + 1, :]
        cnt = cnt + jnp.where((g_n > gate) | ((g_n == gate) & (n < row)), 1, 0)
    sel_ref[0] = jnp.where(cnt < MOBA_TOPK, 0.0, NEG_INF)


def sample_select(q_hq, ksum_h):
    b, nh, tq, dh = q_hq.shape
    nblk = ksum_h.shape[2]
    return pl.pallas_call(
        _sample_select_kernel,
        grid=(b,),
        in_specs=[pl.BlockSpec((1, nh, tq, dh), lambda i: (i, 0, 0, 0)),
                  pl.BlockSpec((1, nh, nblk, dh), lambda i: (i, 0, 0, 0))],
        out_specs=pl.BlockSpec((1, nblk, nh * tq), lambda i: (i, 0, 0)),
        out_shape=jax.ShapeDtypeStruct((b, nblk, nh * tq), F32),
        compiler_params=_params("parallel"),
        name="sample_select",
    )(q_hq, ksum_h)


def _sample_bias_kernel(rt_ref, tiles_ref, new_ref, *, n_heads, tq, page):
    table = lambda b: rt_ref[:, b:b + 1]
    ri = lax.broadcasted_iota(jnp.int32, tiles_ref.shape[1:], 0)
    ci = lax.broadcasted_iota(jnp.int32, tiles_ref.shape[1:], 1)
    same = ci % n_heads == ri // tq
    tiles_ref[0] = jnp.where(same, table(REL_BUCKETS - 1), NEG_INF)
    tiles_ref[1] = jnp.where(same, _rel_bias_of_dist(page + ri % tq - ci // n_heads, table), NEG_INF)
    ri = lax.broadcasted_iota(jnp.int32, new_ref.shape, 0)
    ci = lax.broadcasted_iota(jnp.int32, new_ref.shape, 1)
    d = ri % tq - ci // n_heads
    ok = (d >= 0) & (ci % n_heads == ri // tq)
    new_ref[...] = jnp.where(ok, _rel_bias_of_dist(d, table), NEG_INF)


def sample_bias(rel_bias, tq, page):
    nh = rel_bias.shape[1]
    rows = nh * tq
    rt = jnp.repeat(rel_bias.T, tq, axis=0)
    return pl.pallas_call(
        functools.partial(_sample_bias_kernel, n_heads=nh, tq=tq, page=page),
        out_shape=[jax.ShapeDtypeStruct((2, rows, page * nh), F32), jax.ShapeDtypeStruct((rows, rows), F32)],
        compiler_params=pltpu.CompilerParams(vmem_limit_bytes=V7X_VMEM_LIMIT),
        name="sample_bias",
    )(rt)


def _moba_sample_kernel(pt_ref, q_ref, kn_ref, vn_ref, *rest, ppb):
    k_refs, v_refs, tile_refs = rest[:ppb], rest[ppb:2 * ppb], rest[2 * ppb:3 * ppb]
    sel_ref, new_ref, o_ref, m_ref, l_ref, acc_ref = rest[3 * ppb:]
    p = pl.program_id(1)
    scale = ATTN_HEAD_DIM ** -0.5
    q = q_ref[0]
    rows = q.shape[0]

    @pl.when(p == 0)
    def _():
        lm = _dot_nt(q, kn_ref[0]) * scale + new_ref[...]
        m = jnp.max(lm, axis=-1, keepdims=True)
        pr = jnp.exp(lm - m)
        m_ref[...] = m
        l_ref[...] = jnp.sum(pr, axis=-1, keepdims=True)
        acc_ref[...] = _dot(pr.astype(BF16), vn_ref[0])

    ri = lax.broadcasted_iota(jnp.int32, (rows, rows), 0)
    ci = lax.broadcasted_iota(jnp.int32, (rows, rows), 1)
    picked = jnp.sum(jnp.where(ri == ci, sel_ref[0, 0], 0.0), axis=-1, keepdims=True)
    lm = [_dot_nt(q, k_refs[i][0].astype(BF16)) * scale + tile_refs[i][0] + picked for i in range(ppb)]
    m_old = m_ref[...]
    m_new = m_old
    for x in lm:
        m_new = jnp.maximum(m_new, jnp.max(x, axis=-1, keepdims=True))
    alpha = jnp.exp(m_old - m_new)
    pr = [jnp.exp(x - m_new) for x in lm]
    m_ref[...] = m_new
    l_ref[...] = alpha * l_ref[...] + sum(jnp.sum(x, axis=-1, keepdims=True) for x in pr)
    acc_ref[...] = alpha * acc_ref[...] + sum(
        _dot(pr[i].astype(BF16), v_refs[i][0].astype(BF16)) for i in range(ppb))

    @pl.when(p == pl.num_programs(1) - 1)
    def _():
        o_ref[0] = acc_ref[...] / l_ref[...]


def moba_sample(q, k_new, v_new, cache_k, cache_v, page_table, rel_bias, tq):
    m, d = q.shape
    b = m // tq
    n_pages = page_table.shape[1]
    _, page, nh, dh = cache_k.shape
    ppb = MOBA_BLOCK // page
    rows = nh * tq
    past = n_pages * page
    assert MOBA_BLOCK % page == 0 and past % MOBA_BLOCK == 0 and tq <= MOBA_BLOCK and page + 1 >= T5_SATURATION

    q_hq = q.reshape(b, tq, nh, dh).transpose(0, 2, 1, 3)
    ksum = cache_block_ksum(cache_k, page_table).transpose(0, 2, 1, 3)
    nblk = past // MOBA_BLOCK
    sel = sample_select(q_hq, ksum).reshape(b, nblk, 1, rows)
    bias_tiles, bias_new = sample_bias(rel_bias, tq, page)
    q_rows = q_hq.reshape(b, rows, dh)
    kn = k_new.reshape(b, rows, dh)
    vn = v_new.reshape(b, rows, dh)

    cols = page * nh
    n_phys = cache_k.shape[0]
    ck = cache_k.reshape(n_phys, cols, dh)
    cv = cache_v.reshape(n_phys, cols, dh)
    small = pl.BlockSpec((1, rows, dh), lambda bi, p, pt: (bi, 0, 0))
    pages = _page_specs(ppb, (cols, dh))
    far_tile = pl.BlockSpec((1, rows, cols), lambda bi, p, pt: (0, 0, 0))
    end_tile = pl.BlockSpec((1, rows, cols), lambda bi, p, pt: (jnp.where(p == nblk - 1, 1, 0), 0, 0))
    grid_spec = pltpu.PrefetchScalarGridSpec(
        num_scalar_prefetch=1,
        grid=(b, nblk),
        in_specs=[small, small, small] + pages + pages + [far_tile] * (ppb - 1) + [end_tile]
        + [pl.BlockSpec((1, 1, 1, rows), lambda bi, p, pt: (bi, p, 0, 0)),
           pl.BlockSpec((rows, rows), lambda bi, p, pt: (0, 0))],
        out_specs=small,
        scratch_shapes=[pltpu.VMEM((rows, 1), F32), pltpu.VMEM((rows, 1), F32), pltpu.VMEM((rows, dh), F32)],
    )
    o = pl.pallas_call(
        functools.partial(_moba_sample_kernel, ppb=ppb),
        grid_spec=grid_spec,
        out_shape=jax.ShapeDtypeStruct((b, rows, dh), F32),
        compiler_params=_params("parallel", "arbitrary"),
        name="moba_sample",
    )(page_table, q_rows, kn, vn, *([ck] * ppb), *([cv] * ppb), *([bias_tiles] * ppb), sel, bias_new)
    return o.reshape(b, nh, tq, dh).transpose(0, 2, 1, 3).reshape(m, d).astype(BF16)


def rwkv_time_mix(x, seq_len, shift0, wkv0, gain, w):
    m, d = x.shape
    (xr, xw, xk, xv, xa, xg), shift_new = rwkv_mix(x, gain, w["mu"], shift0, seq_len)
    r = matmul(xr, w["w_r"])
    k = matmul(xk, w["w_k"])
    v = matmul(xv, w["w_v"])
    wl = lora(xw, w["w1"], w["w2"], "tanh")
    al = lora(xa, w["a1"], w["a2"], "none")
    g = lora(xg, w["g1"], w["g2"], "sigmoid")
    y, h_fin = wkv(r, k, v, wl, al, g, w["prm"], jnp.swapaxes(wkv0, -1, -2), seq_len)
    x_new = matmul(y, w["w_o"], res=x)
    return x_new, jnp.swapaxes(h_fin, -1, -2), shift_new


def _rwkv_weights(i, rwkv_mu, rwkv_w_r, rwkv_w_k, rwkv_w_v, rwkv_w_o, rwkv_w0, rwkv_w1, rwkv_w2, rwkv_a0, rwkv_a1,
                  rwkv_a2, rwkv_g1, rwkv_g2, rwkv_k_k, rwkv_k_a, rwkv_r_k, rwkv_gn_g, rwkv_gn_b):
    d = rwkv_mu.shape[-1]
    w1, w2 = _pad_rank(rwkv_w1[i], rwkv_w2[i])
    a1, a2 = _pad_rank(rwkv_a1[i], rwkv_a2[i])
    g1, g2 = _pad_rank(rwkv_g1[i], rwkv_g2[i])
    prm = jnp.stack([rwkv_w0[i], rwkv_a0[i], rwkv_k_k[i], rwkv_k_a[i], rwkv_r_k[i].reshape(d), rwkv_gn_g[i],
                     rwkv_gn_b[i], jnp.zeros((d,), F32)]).astype(F32)
    return dict(mu=rwkv_mu[i], w_r=rwkv_w_r[i].astype(BF16), w_k=rwkv_w_k[i].astype(BF16),
                w_v=rwkv_w_v[i].astype(BF16), w_o=rwkv_w_o[i].astype(BF16),
                w1=w1, w2=w2, a1=a1, a2=a2, g1=g1, g2=g2, prm=prm)


def kernel(x_prompt, x_sample, state_wkv, state_shift, cache_k, cache_v, page_table, norm_mix, norm_ffn, norm_kv, norm_final, rwkv_mu, rwkv_w_r, rwkv_w_k, rwkv_w_v, rwkv_w_o, rwkv_w0, rwkv_w1, rwkv_w2, rwkv_a0, rwkv_a1, rwkv_a2, rwkv_g1, rwkv_g2, rwkv_k_k, rwkv_k_a, rwkv_r_k, rwkv_gn_g, rwkv_gn_b, kv_w, attn_w_q, attn_w_o, rel_bias, ffn_w_gate, ffn_w_up, ffn_w_down, moe_w_router, moe_b_router, moe_w_gate, moe_w_up, moe_w_down):
    d = x_prompt.shape[-1]
    nh = d // ATTN_HEAD_DIM
    assert norm_mix.shape[0] == 2 and rwkv_mu.shape[0] == 1 and attn_w_q.shape[0] == 1
    assert ffn_w_gate.shape[0] == 1 and moe_w_gate.shape[0] == 1

    w_rwkv = _rwkv_weights(0, rwkv_mu, rwkv_w_r, rwkv_w_k, rwkv_w_v, rwkv_w_o, rwkv_w0, rwkv_w1, rwkv_w2, rwkv_a0,
                           rwkv_a1, rwkv_a2, rwkv_g1, rwkv_g2, rwkv_k_k, rwkv_k_a, rwkv_r_k, rwkv_gn_g, rwkv_gn_b)
    wg, wu, wd = ffn_w_gate[0].astype(BF16), ffn_w_up[0].astype(BF16), ffn_w_down[0].astype(BF16)
    kvw_k, kvw_v = kv_w[:, :d].astype(BF16), kv_w[:, d:].astype(BF16)
    wq, wo = attn_w_q[0].astype(BF16), attn_w_o[0].astype(BF16)
    eg, eu, ed = moe_w_gate[0].astype(BF16), moe_w_up[0].astype(BF16), moe_w_down[0].astype(BF16)
    gains_kvq = jnp.stack([norm_kv, norm_mix[1]])

    def run_group(x3, wkv0, shift0, attend):
        b, t, _ = x3.shape
        x = x3.reshape(b * t, d)
        x, wkv_new, shift_new = rwkv_time_mix(x, t, shift0, wkv0, norm_mix[0], w_rwkv)
        x = ffn(x, norm_ffn[0], wg, wu, wd)
        h_kv, h_q = rmsnorm_multi(x, gains_kvq, [BF16, BF16])
        k, k_bf = matmul(h_kv, kvw_k, out_dtypes=(F32, BF16))
        v, v_bf = matmul(h_kv, kvw_v, out_dtypes=(F32, BF16))
        q = matmul(h_q, wq, out_dtypes=(BF16,))
        o = attend(q, k_bf, v_bf, t)
        x = matmul(o, wo, res=x)
        y = moe_layer(x, norm_ffn[1], norm_final, moe_w_router[0], moe_b_router[0], eg, eu, ed)
        kv_shape = (b, t, nh, ATTN_HEAD_DIM)
        return y.reshape(b, t, d), wkv_new[None], shift_new[None], k.reshape(kv_shape), v.reshape(kv_shape)

    bp = x_prompt.shape[0]
    wkv_zero = jnp.zeros((bp,) + state_wkv.shape[2:], state_wkv.dtype)
    shift_zero = jnp.zeros((bp, d), state_shift.dtype)
    out_p = run_group(
        x_prompt, wkv_zero, shift_zero,
        lambda q, k, v, t: moba_prompt(q, k, v, rel_bias, t))
    out_s = run_group(
        x_sample, state_wkv[0], state_shift[0],
        lambda q, k, v, t: moba_sample(q, k, v, cache_k, cache_v, page_table, rel_bias, t))
    return (out_p[0], out_s[0]) + out_p[1:] + out_s[1:]
```

```python
import functools
import math

import jax
import jax.numpy as jnp
from jax import lax
from jax.experimental import pallas as pl
from jax.experimental.pallas import tpu as pltpu

F32 = jnp.float32
BF16 = jnp.bfloat16

RWKV_HEAD = 64
ATTN_HEAD_DIM = 128
MOBA_BLOCK = 256
MOBA_TOPK = 3
REL_BUCKETS = 32
REL_MAX_DIST = 128
MOE_TOPK = 2
GN_EPS = 64e-5
RMS_EPS = 1e-6
NEG_INF = -1e30

V7X_VMEM_LIMIT = 48 * 1024 * 1024
LANES = 128

_HI = lax.Precision.HIGHEST


def _params(*sem):
    return pltpu.CompilerParams(dimension_semantics=sem, vmem_limit_bytes=V7X_VMEM_LIMIT)


def _dot(a, b):
    return jnp.dot(a, b, preferred_element_type=F32)


def _dot_hi(a, b):
    return jnp.dot(a, b, preferred_element_type=F32, precision=_HI)


def _dot_nt(a, b, precision=None):
    return lax.dot_general(a, b, (((1,), (1,)), ((), ())), preferred_element_type=F32, precision=precision)


def _pick(n, prefs):
    for p in prefs:
        if n % p == 0:
            return p
    return n


def _mm_kernel(x_ref, w_ref, *rest, has_res):
    res_ref = rest[0] if has_res else None
    o_refs = rest[1:] if has_res else rest
    acc = _dot(x_ref[...], w_ref[...])
    if has_res:
        acc = acc + res_ref[...]
    for o_ref in o_refs:
        o_ref[...] = acc.astype(o_ref.dtype)


def matmul(x, w, res=None, out_dtypes=(F32,)):
    m, k = x.shape
    n = w.shape[1]
    tm = _pick(m, (1024, 512, 256, 128, 64, 32, 16, 8))
    tn = _pick(n, (512, 256, 128))
    in_specs = [pl.BlockSpec((tm, k), lambda i, j: (i, 0)), pl.BlockSpec((k, tn), lambda i, j: (0, j))]
    args = [x, w]
    if res is not None:
        in_specs.append(pl.BlockSpec((tm, tn), lambda i, j: (i, j)))
        args.append(res)
    outs = pl.pallas_call(
        functools.partial(_mm_kernel, has_res=res is not None),
        grid=(m // tm, n // tn),
        in_specs=in_specs,
        out_specs=[pl.BlockSpec((tm, tn), lambda i, j: (i, j)) for _ in out_dtypes],
        out_shape=[jax.ShapeDtypeStruct((m, n), dt) for dt in out_dtypes],
        compiler_params=_params("parallel", "arbitrary"),
        name="matmul",
    )(*args)
    return outs[0] if len(out_dtypes) == 1 else outs


def _rms(xf):
    return xf * lax.rsqrt(jnp.mean(xf * xf, axis=-1, keepdims=True) + RMS_EPS)


def _norm_kernel(x_ref, g_ref, *o_refs):
    y = _rms(x_ref[...])
    for j, o_ref in enumerate(o_refs):
        o_ref[...] = (y * g_ref[j:j + 1, :]).astype(o_ref.dtype)


def rmsnorm_multi(x, gains, out_dtypes):
    m, d = x.shape
    n = gains.shape[0]
    tm = _pick(m, (512, 256, 128, 64, 32, 16, 8))
    return pl.pallas_call(
        _norm_kernel,
        grid=(m // tm,),
        in_specs=[pl.BlockSpec((tm, d), lambda i: (i, 0)), pl.BlockSpec((n, d), lambda i: (0, 0))],
        out_specs=[pl.BlockSpec((tm, d), lambda i: (i, 0)) for _ in range(n)],
        out_shape=[jax.ShapeDtypeStruct((m, d), dt) for dt in out_dtypes],
        compiler_params=_params("parallel"),
        name="rmsnorm",
    )(x, gains)


def _mix_kernel(x_ref, g_ref, mu_ref, s0_ref, *rest, blocks_per_seq):
    o_refs, last_ref, carry_ref = rest[:6], rest[6], rest[7]
    i = pl.program_id(0)
    h = _rms(x_ref[...]) * g_ref[...]
    tm = h.shape[0]

    @pl.when(i % blocks_per_seq == 0)
    def _():
        carry_ref[...] = s0_ref[0]

    row = lax.broadcasted_iota(jnp.int32, h.shape, 0)
    prev = jnp.where(row == 0, carry_ref[...], pltpu.roll(h, 1, 0))
    xx = prev - h
    for j in range(6):
        o_refs[j][...] = (h + xx * mu_ref[j:j + 1, :]).astype(o_refs[j].dtype)
    last = h[tm - 1:tm, :]
    carry_ref[...] = last
    last_ref[0] = last


def rwkv_mix(x, gain, mu, shift0, seq_len):
    m, d = x.shape
    b = m // seq_len
    tm = _pick(seq_len, (256, 128, 64, 32, 16, 8))
    bps = seq_len // tm
    outs = pl.pallas_call(
        functools.partial(_mix_kernel, blocks_per_seq=bps),
        grid=(m // tm,),
        in_specs=[
            pl.BlockSpec((tm, d), lambda i: (i, 0)),
            pl.BlockSpec((1, d), lambda i: (0, 0)),
            pl.BlockSpec((6, d), lambda i: (0, 0)),
            pl.BlockSpec((1, 1, d), lambda i: (i // bps, 0, 0)),
        ],
        out_specs=[pl.BlockSpec((tm, d), lambda i: (i, 0)) for _ in range(6)]
        + [pl.BlockSpec((1, 1, d), lambda i: (i // bps, 0, 0))],
        out_shape=[jax.ShapeDtypeStruct((m, d), BF16) for _ in range(6)]
        + [jax.ShapeDtypeStruct((b, 1, d), F32)],
        scratch_shapes=[pltpu.VMEM((1, d), F32)],
        compiler_params=_params("arbitrary"),
        name="rwkv_mix",
    )(x, gain.reshape(1, d), mu, shift0.reshape(b, 1, d))
    return outs[:6], outs[6].reshape(b, d)


def _lora_kernel(x_ref, w1_ref, w2_ref, o_ref, *, act):
    t = _dot(x_ref[...], w1_ref[...])
    if act == "tanh":
        t = jnp.tanh(t)
    elif act == "sigmoid":
        t = 1.0 / (1.0 + jnp.exp(-t))
    o_ref[...] = _dot(t.astype(BF16), w2_ref[...])


def lora(x, w1, w2, act):
    m, d = x.shape
    r = w1.shape[1]
    n = w2.shape[1]
    tm = _pick(m, (512, 256, 128, 64, 32, 16, 8))
    return pl.pallas_call(
        functools.partial(_lora_kernel, act=act),
        grid=(m // tm,),
        in_specs=[
            pl.BlockSpec((tm, d), lambda i: (i, 0)),
            pl.BlockSpec((d, r), lambda i: (0, 0)),
            pl.BlockSpec((r, n), lambda i: (0, 0)),
        ],
        out_specs=pl.BlockSpec((tm, n), lambda i: (i, 0)),
        out_shape=jax.ShapeDtypeStruct((m, n), F32),
        compiler_params=_params("parallel"),
        name="lora",
    )(x, w1, w2)


def _pad_rank(w1, w2):
    r = w1.shape[1]
    rp = -(-r // LANES) * LANES
    if rp != r:
        w1 = jnp.pad(w1, ((0, 0), (0, rp - r)))
        w2 = jnp.pad(w2, ((0, rp - r), (0, 0)))
    return w1.astype(BF16), w2.astype(BF16)


def _split(a):
    hi = a.astype(BF16)
    return hi, (a - hi.astype(F32)).astype(BF16)


def _dot3(a, b, mode):
    (ah, al), (bh, bl) = a, b
    la = 0 if mode == "tn" else 1
    lb = 1 if mode == "nt" else 0
    lhs = jnp.concatenate([ah, ah, al], axis=la)
    rhs = jnp.concatenate([bh, bl, bh], axis=lb)
    return lax.dot_general(lhs, rhs, (((la,), (lb,)), ((), ())), preferred_element_type=F32)


def _wkv_kernel(r_ref, k_ref, v_ref, wl_ref, al_ref, g_ref, prm_ref, h0_ref, y_ref, hout_ref, h_scr,
                *, chunk, pairs):
    c_idx = pl.program_id(2)
    n = RWKV_HEAD
    pw = 2 * n
    cw = 2 * chunk
    tb = r_ref.shape[0]
    n_chunks = tb // chunk

    def iota(shape, dim):
        return lax.broadcasted_iota(jnp.int32, shape, dim)

    def block_diag(x, w):
        even = iota((1, 2 * w), 1) < w
        return jnp.concatenate([jnp.where(even, x, 0.0), jnp.where(even, 0.0, x)], axis=0)

    def split_bd(x, w):
        even = jnp.where(iota((1, 2 * w), 1) < w, 1.0, 0.0).astype(BF16)
        hi, lo = _split(x)
        return tuple(jnp.concatenate([t * even, t * (1 - even)], axis=0) for t in (hi, lo))

    def cat_rows(a, b):
        return tuple(jnp.concatenate([s, t], axis=0) for s, t in zip(a, b))

    @pl.when(c_idx == 0)
    def _():
        for p in range(pairs):
            h_scr[p] = block_diag(jnp.concatenate([h0_ref[0, 2 * p], h0_ref[0, 2 * p + 1]], axis=1), n)

    ri = iota((chunk, cw), 0)
    ci = iota((chunk, cw), 1) % chunk
    strict = ri > ci
    incl = ri >= ci
    eye_c = jnp.where(ri == ci, 1.0, 0.0)
    tri = jnp.where(iota((chunk, chunk), 0) >= iota((chunk, chunk), 1), 1.0, 0.0).astype(BF16)
    tri3 = jnp.concatenate([tri, tri, tri], axis=1)
    same_head = iota((pw, pw), 0) // n == iota((pw, pw), 1) // n
    seg = jnp.where(same_head, 1.0, 0.0).astype(BF16)
    seg2 = jnp.concatenate([seg, seg], axis=0)
    eye_pw = jnp.where(iota((pw, pw), 0) == iota((pw, pw), 1), 1.0, 0.0)
    n_double = max(int(math.log2(chunk)) - 1, 0)

    def head_sum(x):
        x_h, x_l = _split(x)
        return _dot(jnp.concatenate([x_h, x_l], axis=1), seg2)

    def one_chunk(ci_, hs):
        rows = pl.ds(pl.multiple_of(ci_ * chunk, chunk), chunk)
        r_all = r_ref[rows, :]
        k_all = k_ref[rows, :]
        v_all = v_ref[rows, :]
        wl_all = wl_ref[rows, :]
        al_all = al_ref[rows, :]
        pr = range(pairs)
        sls = [slice(j * pw, (j + 1) * pw) for j in pr]
        prm = lambda row, j: prm_ref[row:row + 1, sls[j]]
        r = [r_all[:, sl] for sl in sls]
        k = [k_all[:, sl] for sl in sls]
        v = [v_all[:, sl] for sl in sls]
        z = [-(prm(0, j) + wl_all[:, sls[j]]) for j in pr]
        ld = [-jnp.exp(-(jnp.maximum(zj, 0.0) + jnp.log(1.0 + jnp.exp(-jnp.abs(zj)))) - 0.5) for zj in z]
        alpha = [1.0 / (1.0 + jnp.exp(-(prm(1, j) + al_all[:, sls[j]]))) for j in pr]
        kkv = [k[j] * prm(2, j) for j in pr]
        kk = [kkv[j] / jnp.maximum(jnp.sqrt(head_sum(kkv[j] * kkv[j])), 1e-12) for j in pr]
        k2 = [k[j] * (1.0 + (alpha[j] - 1.0) * prm(3, j)) for j in pr]

        def ld3(x):
            x_h, x_m = _split(x)
            return jnp.concatenate([x_h, x_m, (x - x_h.astype(F32) - x_m.astype(F32)).astype(BF16)], axis=0)

        cum = [_dot(tri3, ld3(ld[j])) for j in pr]
        w_in = [jnp.exp(c) for c in cum]
        w_inv = [jnp.exp(-c) for c in cum]
        w_tot = [w[chunk - 1:chunk, :] for w in w_in]
        ar = [_split(jnp.concatenate([-kk[j] * jnp.exp(cum[j] - ld[j]), r[j] * w_in[j]], axis=0)) for j in pr]
        b_t = [kk[j] * alpha[j] * w_inv[j] for j in pr]
        k_t = [k2[j] * w_inv[j] for j in pr]
        bk_bd = [cat_rows(split_bd(b_t[j], n), split_bd(k_t[j], n)) for j in pr]
        big = [_dot3(ar[j], bk_bd[j], "nt") for j in pr]
        a_ab = [jnp.where(strict, b_[:chunk, :cw], 0.0) for b_ in big]
        a_ak = [jnp.where(strict, b_[:chunk, cw:], 0.0) for b_ in big]
        a_r = [jnp.concatenate([jnp.where(incl, b_[chunk:, :cw], 0.0),
                                jnp.where(incl, b_[chunk:, cw:], 0.0)], axis=1) for b_ in big]
        tinv = [eye_c + a for a in a_ab]
        p = a_ab
        for _ in range(n_double):
            p = [_dot3(_split(x), split_bd(x, chunk), "nn") for x in p]
            tinv = [tinv[j] + _dot3(_split(tinv[j]), split_bd(p[j], chunk), "nn") for j in pr]

        hdep = [_dot3(ar[j], _split(hs[j]), "nn") for j in pr]
        v_bd = [split_bd(v[j], n) for j in pr]
        akv = [_dot3(_split(a_ak[j]), v_bd[j], "nn") for j in pr]
        u = [_dot3(_split(tinv[j]), split_bd(hdep[j][:chunk] + akv[j], n), "nn") for j in pr]
        uv_bd = [cat_rows(split_bd(u[j], n), v_bd[j]) for j in pr]
        o = [hdep[j][chunk:] + _dot3(_split(a_r[j]), uv_bd[j], "nn") for j in pr]
        w_col = [jnp.sum(eye_pw * w_tot[j], axis=1, keepdims=True) for j in pr]
        bk_s = [_split(jnp.concatenate([b_t[j], k_t[j]], axis=0) * w_tot[j]) for j in pr]
        uv = [_split(jnp.concatenate([u[j], v[j]], axis=0)) for j in pr]
        h_new = [hs[j] * w_col[j] + jnp.where(same_head, _dot3(bk_s[j], uv[j], "tn"), 0.0) for j in pr]

        outs = []
        for j in pr:
            d_ = o[j] - head_sum(o[j]) * (1.0 / n)
            var = head_sum(d_ * d_) * (1.0 / n)
            on = d_ * lax.rsqrt(var + GN_EPS) * prm(5, j) + prm(6, j)
            outs.append(on + head_sum(r[j] * k2[j] * prm(4, j)) * v[j])
        y_ref[rows, :] = (jnp.concatenate(outs, axis=1) * g_ref[rows, :]).astype(y_ref.dtype)
        return tuple(h_new)

    hs = lax.fori_loop(0, n_chunks, one_chunk, tuple(h_scr[j] for j in range(pairs)))
    for j in range(pairs):
        h_scr[j] = hs[j]

    @pl.when(c_idx == pl.num_programs(2) - 1)
    def _():
        for j in range(pairs):
            hout_ref[0, 2 * j] = hs[j][:n, :n]
            hout_ref[0, 2 * j + 1] = hs[j][n:, n:]


def wkv(r, k, v, wl, al, g, prm, h0, seq_len):
    m, d = r.shape
    b = m // seq_len
    n = RWKV_HEAD
    nh = d // n
    heads = _pick(nh, (16, 8, 4, 2))
    assert nh % heads == 0 and heads % 2 == 0
    chunk = min(64, seq_len)
    tb = _pick(seq_len, (512, 256, 128, 64, 32, 16, 8))
    nb = seq_len // tb
    wd = heads * n
    row_spec = pl.BlockSpec((tb, wd), lambda bi, hi, ci: (bi * nb + ci, hi))
    st_spec = pl.BlockSpec((1, heads, n, n), lambda bi, hi, ci: (bi, hi, 0, 0))
    return pl.pallas_call(
        functools.partial(_wkv_kernel, chunk=chunk, pairs=heads // 2),
        grid=(b, nh // heads, nb),
        in_specs=[row_spec] * 6 + [pl.BlockSpec((8, wd), lambda bi, hi, ci: (0, hi)), st_spec],
        out_specs=[row_spec, st_spec],
        out_shape=[jax.ShapeDtypeStruct((m, d), BF16), jax.ShapeDtypeStruct((b, nh, n, n), F32)],
        scratch_shapes=[pltpu.VMEM((heads // 2, 2 * n, 2 * n), F32)],
        compiler_params=_params("parallel", "parallel", "arbitrary"),
        name="wkv",
    )(r, k, v, wl, al, g, prm, h0)


def _silu_mul(g, u):
    return g * (1.0 / (1.0 + jnp.exp(-g))) * u


def _ffn_kernel(x_ref, g_ref, wg_ref, wu_ref, wd_ref, o_ref, h_ref, acc_ref):
    j = pl.program_id(1)

    @pl.when(j == 0)
    def _():
        x = x_ref[...]
        acc_ref[...] = x
        h_ref[...] = (_rms(x) * g_ref[...]).astype(BF16)

    h = h_ref[...]
    a = _silu_mul(_dot(h, wg_ref[...]), _dot(h, wu_ref[...])).astype(BF16)
    acc_ref[...] += _dot(a, wd_ref[...])

    @pl.when(j == pl.num_programs(1) - 1)
    def _():
        o_ref[...] = acc_ref[...]


def ffn(x, gain, wg, wu, wd):
    m, d = x.shape
    f = wg.shape[1]
    tm = _pick(m, (512, 256, 128, 64, 32, 16, 8))
    tf = _pick(f, (512, 256, 128))
    return pl.pallas_call(
        _ffn_kernel,
        grid=(m // tm, f // tf),
        in_specs=[
            pl.BlockSpec((tm, d), lambda i, j: (i, 0)),
            pl.BlockSpec((1, d), lambda i, j: (0, 0)),
            pl.BlockSpec((d, tf), lambda i, j: (0, j)),
            pl.BlockSpec((d, tf), lambda i, j: (0, j)),
            pl.BlockSpec((tf, d), lambda i, j: (j, 0)),
        ],
        out_specs=pl.BlockSpec((tm, d), lambda i, j: (i, 0)),
        out_shape=jax.ShapeDtypeStruct((m, d), F32),
        scratch_shapes=[pltpu.VMEM((tm, d), BF16), pltpu.VMEM((tm, d), F32)],
        compiler_params=_params("parallel", "arbitrary"),
        name="ffn",
    )(x, gain.reshape(1, d), wg, wu, wd)


def _moe_ffn_kernel(te_ref, tv_ref, src_ref, x_hbm, wg_ref, wu_ref, wd_ref, o_ref, xbuf, acc_ref, sems,
                    *, per_step, n_steps):
    i = pl.program_id(0)
    j = pl.program_id(1)
    nt = pl.num_programs(0)
    tm = o_ref.shape[0]
    n_req = per_step * n_steps
    slot = i % 2

    def request(tile, row, s):
        src_row = src_ref[tile * tm + jnp.minimum(row, tm - 1)]
        return _row_copy(x_hbm, xbuf.at[s], sems.at[s], src_row, row)

    def wait_tile(s):
        for r in range(n_req):
            _row_copy(x_hbm, xbuf.at[s], sems.at[s], 0, r).wait()

    @pl.when((i == 0) & (j == 0))
    def _():
        def start(r, c):
            request(0, r, 0).start()
            return c
        lax.fori_loop(0, n_req, start, 0)

    def request_next():
        nxt = jnp.minimum(i + 1, nt - 1)
        for r in range(per_step):
            request(nxt, j * per_step + r, 1 - slot).start()

    @pl.when(j == 0)
    def _():
        wait_tile(slot)

    @pl.when(tv_ref[i] > 0)
    def _():
        @pl.when(j == 0)
        def _():
            acc_ref[...] = jnp.zeros_like(acc_ref)

        request_next()
        x = xbuf[slot, 0:tm, :].astype(BF16)
        a = _silu_mul(_dot(x, wg_ref[0]), _dot(x, wu_ref[0])).astype(BF16)
        acc_ref[...] += _dot(a, wd_ref[0])

        @pl.when(j == pl.num_programs(1) - 1)
        def _():
            o_ref[...] = acc_ref[...]

    @pl.when(tv_ref[i] == 0)
    def _():
        request_next()
        o_ref[...] = jnp.zeros_like(o_ref)

    @pl.when((i == nt - 1) & (j == n_steps - 1))
    def _():
        wait_tile(1 - slot)


def moe_ffn(x, src_token, wg, wu, wd, tile_expert, tile_valid, tm):
    d = x.shape[1]
    mp = src_token.shape[0]
    f = wg.shape[2]
    tf = _pick(f, (512, 256, 128))
    nf = f // tf
    per_step = -(-tm // nf)
    tm_buf = -(-per_step * nf // 8) * 8

    def fj(i, j, tv):
        return jnp.where(tv[i] > 0, j, nf - 1)

    grid_spec = pltpu.PrefetchScalarGridSpec(
        num_scalar_prefetch=3,
        grid=(mp // tm, nf),
        in_specs=[
            pl.BlockSpec(memory_space=pl.ANY),
            pl.BlockSpec((1, d, tf), lambda i, j, te, tv, src: (te[i], 0, fj(i, j, tv))),
            pl.BlockSpec((1, d, tf), lambda i, j, te, tv, src: (te[i], 0, fj(i, j, tv))),
            pl.BlockSpec((1, tf, d), lambda i, j, te, tv, src: (te[i], fj(i, j, tv), 0)),
        ],
        out_specs=pl.BlockSpec((tm, d), lambda i, j, te, tv, src: (i, 0)),
        scratch_shapes=[pltpu.VMEM((2, tm_buf, d), F32), pltpu.VMEM((tm, d), F32), pltpu.SemaphoreType.DMA((2,))],
    )
    return pl.pallas_call(
        functools.partial(_moe_ffn_kernel, per_step=per_step, n_steps=nf),
        grid_spec=grid_spec,
        out_shape=jax.ShapeDtypeStruct((mp, d), F32),
        compiler_params=_params("arbitrary", "arbitrary"),
        name="moe_ffn",
    )(tile_expert, tile_valid, src_token, x, wg, wu, wd)


def _router_kernel(x_ref, g_ref, wr_ref, br_ref, h_ref, rt_ref):
    h = _rms(x_ref[...]) * g_ref[...]
    h_ref[...] = h
    logits = _dot_hi(h, wr_ref[...]) + br_ref[...]
    lane = lax.broadcasted_iota(jnp.int32, logits.shape, 1)
    m1 = jnp.max(logits, axis=-1, keepdims=True)
    i1 = jnp.min(jnp.where(logits == m1, lane, LANES), axis=-1, keepdims=True)
    rest = jnp.where(lane == i1, -jnp.inf, logits)
    m2 = jnp.max(rest, axis=-1, keepdims=True)
    i2 = jnp.min(jnp.where(rest == m2, lane, LANES), axis=-1, keepdims=True)
    e = jnp.exp(m2 - m1)
    g1 = 1.0 / (1.0 + e)
    g2 = e / (1.0 + e)
    rt_ref[...] = jnp.where(lane == 0, g1, jnp.where(lane == 1, g2, jnp.where(
        lane == 2, i1.astype(F32), jnp.where(lane == 3, i2.astype(F32), 0.0))))


def router(x, gain, w_router, b_router):
    m, d = x.shape
    e = w_router.shape[1]
    wr = jnp.pad(w_router.astype(F32), ((0, 0), (0, LANES - e)))
    br = jnp.pad(b_router.astype(F32), (0, LANES - e), constant_values=NEG_INF).reshape(1, LANES)
    tm = _pick(m, (512, 256, 128, 64, 32, 16, 8))
    return pl.pallas_call(
        _router_kernel,
        grid=(m // tm,),
        in_specs=[
            pl.BlockSpec((tm, d), lambda i: (i, 0)),
            pl.BlockSpec((1, d), lambda i: (0, 0)),
            pl.BlockSpec((d, LANES), lambda i: (0, 0)),
            pl.BlockSpec((1, LANES), lambda i: (0, 0)),
        ],
        out_specs=[pl.BlockSpec((tm, d), lambda i: (i, 0)), pl.BlockSpec((tm, LANES), lambda i: (i, 0))],
        out_shape=[jax.ShapeDtypeStruct((m, d), F32), jax.ShapeDtypeStruct((m, LANES), F32)],
        compiler_params=_params("parallel"),
        name="router",
    )(x, gain.reshape(1, d), wr, br)


def _row_copy(src_hbm, dst_ref, sem, src_row, dst_row):
    return pltpu.make_async_copy(src_hbm.at[pl.ds(src_row, 1)], dst_ref.at[pl.ds(dst_row, 1)], sem)


def _combine_kernel(p1_ref, p2_ref, y_hbm, x_ref, rt_ref, g_ref, o_ref, b1_ref, b2_ref, sem1, sem2):
    t = pl.program_id(0)
    tr = x_ref.shape[0]
    slot = t % 2

    def request(tile, s):
        for r in range(tr):
            _row_copy(y_hbm, b1_ref.at[s], sem1.at[s], p1_ref[tile * tr + r], r).start()
            _row_copy(y_hbm, b2_ref.at[s], sem2.at[s], p2_ref[tile * tr + r], r).start()

    @pl.when(t == 0)
    def _():
        request(0, 0)

    @pl.when(t + 1 < pl.num_programs(0))
    def _():
        request(t + 1, 1 - slot)

    for r in range(tr):
        _row_copy(y_hbm, b1_ref.at[slot], sem1.at[slot], 0, r).wait()
        _row_copy(y_hbm, b2_ref.at[slot], sem2.at[slot], 0, r).wait()
    rt = rt_ref[...]
    x = x_ref[...] + rt[:, 0:1] * b1_ref[slot] + rt[:, 1:2] * b2_ref[slot]
    o_ref[...] = _rms(x) * g_ref[...]


def moe_combine(x, y_sorted, rt, pos1, pos2, gain):
    m, d = x.shape
    tr = _pick(m, (128, 64, 32, 16, 8))
    grid_spec = pltpu.PrefetchScalarGridSpec(
        num_scalar_prefetch=2,
        grid=(m // tr,),
        in_specs=[
            pl.BlockSpec(memory_space=pl.ANY),
            pl.BlockSpec((tr, d), lambda i, p1, p2: (i, 0)),
            pl.BlockSpec((tr, LANES), lambda i, p1, p2: (i, 0)),
            pl.BlockSpec((1, d), lambda i, p1, p2: (0, 0)),
        ],
        out_specs=pl.BlockSpec((tr, d), lambda i, p1, p2: (i, 0)),
        scratch_shapes=[pltpu.VMEM((2, tr, d), F32), pltpu.VMEM((2, tr, d), F32),
                        pltpu.SemaphoreType.DMA((2,)), pltpu.SemaphoreType.DMA((2,))],
    )
    return pl.pallas_call(
        _combine_kernel,
        grid_spec=grid_spec,
        out_shape=jax.ShapeDtypeStruct((m, d), F32),
        compiler_params=_params("arbitrary"),
        name="moe_combine",
    )(pos1, pos2, y_sorted, x, rt, gain.reshape(1, d))


def moe_layer(x, gain_ffn, gain_final, w_router, b_router, wg, wu, wd):
    m, d = x.shape
    n_exp = wg.shape[0]
    h, rt = router(x, gain_ffn, w_router, b_router)
    tm = 512 if m * MOE_TOPK >= 8 * 512 else 128
    experts = rt[:, 2:4].astype(jnp.int32).reshape(-1)
    n_asg = experts.shape[0]
    counts = jnp.sum(experts[:, None] == jnp.arange(n_exp)[None, :], axis=0)
    padded = ((counts + tm - 1) // tm) * tm
    start = jnp.cumsum(counts) - counts
    pstart = jnp.cumsum(padded) - padded
    order = jnp.argsort(experts, stable=True)
    rank = jnp.zeros((n_asg,), jnp.int32).at[order].set(jnp.arange(n_asg, dtype=jnp.int32))
    dest = (pstart[experts] + rank - start[experts]).astype(jnp.int32)
    mp = n_asg + n_exp * tm
    src_token = jnp.zeros((mp,), jnp.int32).at[dest].set(jnp.arange(n_asg, dtype=jnp.int32) // MOE_TOPK)
    tile_start = jnp.arange(mp // tm, dtype=jnp.int32) * tm
    pend = jnp.cumsum(padded)
    tile_expert = jnp.minimum(jnp.sum(tile_start[:, None] >= pend[None, :], axis=1), n_exp - 1).astype(jnp.int32)
    tile_valid = (tile_start < pend[-1]).astype(jnp.int32)
    last_used = jnp.max(jnp.where(tile_valid > 0, tile_expert, 0))
    tile_expert = jnp.where(tile_valid > 0, tile_expert, last_used)

    y_sorted = moe_ffn(h, src_token, wg, wu, wd, tile_expert, tile_valid, tm)
    pos = dest.reshape(m, MOE_TOPK)
    return moe_combine(x, y_sorted, rt, pos[:, 0], pos[:, 1], gain_final)


_T5_EXACT = REL_BUCKETS // 2
_T5_STEPS = [_T5_EXACT] + [
    math.ceil(_T5_EXACT * (REL_MAX_DIST / _T5_EXACT) ** (n / (REL_BUCKETS - _T5_EXACT)))
    for n in range(1, REL_BUCKETS - _T5_EXACT)]
T5_SATURATION = _T5_STEPS[-1]


def _rel_bias_of_dist(d, table):
    val = jnp.where(d >= 1, table(1), table(0))
    for b in range(2, _T5_EXACT):
        val = jnp.where(d >= b, table(b), val)
    for n, t in enumerate(_T5_STEPS):
        val = jnp.where(d >= t, table(_T5_EXACT + n), val)
    return val


def _prompt_bias_kernel(rb_ref, dg_ref, aj_ref):
    h = pl.program_id(0)
    blk = dg_ref.shape[1]
    ri = lax.broadcasted_iota(jnp.int32, (blk, blk), 0)
    ci = lax.broadcasted_iota(jnp.int32, (blk, blk), 1)
    table = lambda b: rb_ref[h, b]
    far = table(REL_BUCKETS - 1)
    dg_ref[0] = jnp.where(ri >= ci, _rel_bias_of_dist(ri - ci, table) - far, NEG_INF)
    aj_ref[0] = _rel_bias_of_dist(ri - ci + blk, table) - far


def prompt_bias(rel_bias):
    nh = rel_bias.shape[1]
    blk = MOBA_BLOCK
    tile = pl.BlockSpec((1, blk, blk), lambda h: (h, 0, 0))
    return pl.pallas_call(
        _prompt_bias_kernel,
        grid=(nh,),
        in_specs=[pl.BlockSpec(memory_space=pltpu.SMEM)],
        out_specs=[tile, tile],
        out_shape=[jax.ShapeDtypeStruct((nh, blk, blk), F32)] * 2,
        compiler_params=_params("arbitrary"),
        name="prompt_bias",
    )(rel_bias.T)


_GATE_ROWS = 16


def _moba_prompt_kernel(c_last_ref, q_ref, k_ref, v_ref, dg_ref, aj_ref, o_ref, s_ref, ks_ref):
    h = pl.program_id(1)
    qi = pl.program_id(2)
    blk = MOBA_BLOCK
    nblk = k_ref.shape[0] // blk
    scale = ATTN_HEAD_DIM ** -0.5
    q = q_ref[...]
    c_last = c_last_ref[h]

    @pl.when(qi == 0)
    def _():
        ks_ref[...] = jnp.zeros_like(ks_ref)
        for j in range(nblk):
            ks_ref[j:j + 1, :] = jnp.sum(k_ref[j * blk:(j + 1) * blk, :].astype(F32), axis=0, keepdims=True)

    half_rows = _GATE_ROWS // 2
    ks_hi, ks_lo = _split(ks_ref[...])
    gate = (_dot_nt(ks_hi, q) + _dot_nt(ks_lo, q))[:half_rows]
    row = lax.broadcasted_iota(jnp.int32, gate.shape, 0)
    gate = jnp.where(row < qi, gate, -jnp.inf)
    cnt = jnp.zeros(gate.shape, jnp.int32)
    for n in range(nblk):
        g_n = gate[n:n + 1, :]
        cnt = cnt + jnp.where((g_n > gate) | ((g_n == gate) & (n < row)), 1, 0)
    attend = ((cnt < MOBA_TOPK) & (row < qi)) | (row == qi)
    c_full = jnp.full(gate.shape, c_last, F32)
    c_hi = c_full.astype(BF16).astype(F32)
    mask_t = jnp.concatenate([jnp.where(attend, c_hi, NEG_INF), jnp.where(attend, c_full - c_hi, 0.0)],
                             axis=0).astype(BF16)

    def attend_blocks(nk):
        tk = nk * blk
        kb = lax.broadcasted_iota(jnp.int32, (_GATE_ROWS, tk), 1) // blk
        rb = lax.broadcasted_iota(jnp.int32, (_GATE_ROWS, tk), 0) % half_rows
        expand = jnp.where(kb == rb, 1.0, 0.0).astype(BF16)
        mask = lax.dot_general(mask_t, expand, (((0,), (0,)), ((), ())), preferred_element_type=F32)
        lm = _dot_nt(q, k_ref[0:tk, :]) * scale + mask
        for j in range(nk):
            s_ref[j] = lm[:, j * blk:(j + 1) * blk]
        s_ref[qi] += dg_ref[0]

        @pl.when(qi > 0)
        def _():
            s_ref[qi - 1] += aj_ref[0]

        m = jnp.max(s_ref[0], axis=-1, keepdims=True)
        for j in range(1, nk):
            m = jnp.maximum(m, jnp.max(s_ref[j], axis=-1, keepdims=True))
        p = [jnp.exp(s_ref[j] - m) for j in range(nk)]
        l = sum(jnp.sum(x, axis=-1, keepdims=True) for x in p)
        acc = _dot(jnp.concatenate([x.astype(BF16) for x in p], axis=1), v_ref[0:tk, :])
        o_ref[...] = (acc / l).astype(o_ref.dtype)

    lo = 0
    for nk in range(2, nblk + 2, 2):
        nk = min(nk, nblk)
        pl.when((qi >= lo) & (qi < nk))(functools.partial(attend_blocks, nk))
        lo = nk


def moba_prompt(q, k, v, rel_bias, seq_len):
    m, d = q.shape
    dh = ATTN_HEAD_DIM
    nh = d // dh
    b = m // seq_len
    blk = MOBA_BLOCK
    nq = seq_len // blk
    assert seq_len % blk == 0 and nq <= _GATE_ROWS // 2
    dg, aj = prompt_bias(rel_bias)
    kv_spec = pl.BlockSpec((seq_len, dh), lambda bi, hi, qi: (bi, hi))
    tile = pl.BlockSpec((1, blk, blk), lambda bi, hi, qi: (hi, 0, 0))
    q_spec = pl.BlockSpec((blk, dh), lambda bi, hi, qi: (bi * nq + qi, hi))
    return pl.pallas_call(
        _moba_prompt_kernel,
        grid=(b, nh, nq),
        in_specs=[pl.BlockSpec(memory_space=pltpu.SMEM), q_spec, kv_spec, kv_spec, tile, tile],
        out_specs=q_spec,
        out_shape=jax.ShapeDtypeStruct((m, d), BF16),
        scratch_shapes=[pltpu.VMEM((nq, blk, blk), F32), pltpu.VMEM((_GATE_ROWS, dh), F32)],
        compiler_params=_params("parallel", "parallel", "arbitrary"),
        name="moba_prompt",
    )(rel_bias[REL_BUCKETS - 1], q, k, v, dg, aj)


def _page_specs(n, page_shape):
    zeros = (0,) * len(page_shape)
    return [pl.BlockSpec((1,) + page_shape, lambda bi, p, pt, i=i: (pt[bi, n * p + i],) + zeros) for i in range(n)]


def _ksum_kernel(pt_ref, *refs, pages_per_block):
    k_refs, o_ref = refs[:-1], refs[-1]
    for blk in range(len(k_refs) // pages_per_block):
        s = jnp.sum(k_refs[blk * pages_per_block][0], axis=0)
        for i in range(1, pages_per_block):
            s = s + jnp.sum(k_refs[blk * pages_per_block + i][0], axis=0)
        o_ref[0, blk] = s


def cache_block_ksum(cache_k, page_table):
    b, n_pages = page_table.shape
    _, page, nh, dh = cache_k.shape
    ppb = MOBA_BLOCK // page
    pps = _pick(n_pages, (4 * ppb, 2 * ppb, ppb))
    grid_spec = pltpu.PrefetchScalarGridSpec(
        num_scalar_prefetch=1,
        grid=(b, n_pages // pps),
        in_specs=_page_specs(pps, (page, nh, dh)),
        out_specs=pl.BlockSpec((1, pps // ppb, nh, dh), lambda bi, p, pt: (bi, p, 0, 0)),
    )
    return pl.pallas_call(
        functools.partial(_ksum_kernel, pages_per_block=ppb),
        grid_spec=grid_spec,
        out_shape=jax.ShapeDtypeStruct((b, n_pages // ppb, nh, dh), F32),
        compiler_params=_params("parallel", "arbitrary"),
        name="cache_block_ksum",
    )(page_table, *([cache_k] * pps))


def _sample_select_kernel(q_ref, ks_ref, sel_ref):
    nh = q_ref.shape[1]
    nblk = ks_ref.shape[2]
    gate = jnp.concatenate(
        [_dot_nt(ks_ref[0, h], q_ref[0, h].astype(F32), _HI) for h in range(nh)], axis=1)
    row = lax.broadcasted_iota(jnp.int32, gate.shape, 0)
    cnt = jnp.zeros(gate.shape, jnp.int32)
    for n in range(nblk):
        g_n = gate[n:n + 1, :]
        cnt = cnt + jnp.where((g_n > gate) | ((g_n == gate) & (n < row)), 1, 0)
    sel_ref[0] = jnp.where(cnt < MOBA_TOPK, 0.0, NEG_INF)


def sample_select(q_hq, ksum_h):
    b, nh, tq, dh = q_hq.shape
    nblk = ksum_h.shape[2]
    return pl.pallas_call(
        _sample_select_kernel,
        grid=(b,),
        in_specs=[pl.BlockSpec((1, nh, tq, dh), lambda i: (i, 0, 0, 0)),
                  pl.BlockSpec((1, nh, nblk, dh), lambda i: (i, 0, 0, 0))],
        out_specs=pl.BlockSpec((1, nblk, nh * tq), lambda i: (i, 0, 0)),
        out_shape=jax.ShapeDtypeStruct((b, nblk, nh * tq), F32),
        compiler_params=_params("parallel"),
        name="sample_select",
    )(q_hq, ksum_h)


def _sample_bias_kernel(rt_ref, tiles_ref, new_ref, *, n_heads, tq, page):
    table = lambda b: rt_ref[:, b:b + 1]
    ri = lax.broadcasted_iota(jnp.int32, tiles_ref.shape[1:], 0)
    ci = lax.broadcasted_iota(jnp.int32, tiles_ref.shape[1:], 1)
    same = ci % n_heads == ri // tq
    tiles_ref[0] = jnp.where(same, table(REL_BUCKETS - 1), NEG_INF)
    tiles_ref[1] = jnp.where(same, _rel_bias_of_dist(page + ri % tq - ci // n_heads, table), NEG_INF)
    ri = lax.broadcasted_iota(jnp.int32, new_ref.shape, 0)
    ci = lax.broadcasted_iota(jnp.int32, new_ref.shape, 1)
    d = ri % tq - ci // n_heads
    ok = (d >= 0) & (ci % n_heads == ri // tq)
    new_ref[...] = jnp.where(ok, _rel_bias_of_dist(d, table), NEG_INF)


def sample_bias(rel_bias, tq, page):
    nh = rel_bias.shape[1]
    rows = nh * tq
    rt = jnp.repeat(rel_bias.T, tq, axis=0)
    return pl.pallas_call(
        functools.partial(_sample_bias_kernel, n_heads=nh, tq=tq, page=page),
        out_shape=[jax.ShapeDtypeStruct((2, rows, page * nh), F32), jax.ShapeDtypeStruct((rows, rows), F32)],
        compiler_params=pltpu.CompilerParams(vmem_limit_bytes=V7X_VMEM_LIMIT),
        name="sample_bias",
    )(rt)


def _moba_sample_kernel(pt_ref, q_ref, kn_ref, vn_ref, *rest, ppb):
    k_refs, v_refs, tile_refs = rest[:ppb], rest[ppb:2 * ppb], rest[2 * ppb:3 * ppb]
    sel_ref, new_ref, o_ref, m_ref, l_ref, acc_ref = rest[3 * ppb:]
    p = pl.program_id(1)
    scale = ATTN_HEAD_DIM ** -0.5
    q = q_ref[0]
    rows = q.shape[0]

    @pl.when(p == 0)
    def _():
        lm = _dot_nt(q, kn_ref[0]) * scale + new_ref[...]
        m = jnp.max(lm, axis=-1, keepdims=True)
        pr = jnp.exp(lm - m)
        m_ref[...] = m
        l_ref[...] = jnp.sum(pr, axis=-1, keepdims=True)
        acc_ref[...] = _dot(pr.astype(BF16), vn_ref[0])

    ri = lax.broadcasted_iota(jnp.int32, (rows, rows), 0)
    ci = lax.broadcasted_iota(jnp.int32, (rows, rows), 1)
    picked = jnp.sum(jnp.where(ri == ci, sel_ref[0, 0], 0.0), axis=-1, keepdims=True)
    lm = [_dot_nt(q, k_refs[i][0].astype(BF16)) * scale + tile_refs[i][0] for i in range(ppb)]
    m_old = m_ref[...]
    m_blk = jnp.max(lm[0], axis=-1, keepdims=True)
    for x in lm[1:]:
        m_blk = jnp.maximum(m_blk, jnp.max(x, axis=-1, keepdims=True))
    m_new = jnp.maximum(m_old, m_blk + picked)
    alpha = jnp.exp(m_old - m_new)
    shift = m_new - picked
    pr = [jnp.exp(x - shift) for x in lm]
    m_ref[...] = m_new
    l_ref[...] = alpha * l_ref[...] + sum(jnp.sum(x, axis=-1, keepdims=True) for x in pr)
    acc_ref[...] = alpha * acc_ref[...] + sum(
        _dot(pr[i].astype(BF16), v_refs[i][0].astype(BF16)) for i in range(ppb))

    @pl.when(p == pl.num_programs(1) - 1)
    def _():
        o_ref[0] = acc_ref[...] / l_ref[...]


def moba_sample(q, k_new, v_new, cache_k, cache_v, page_table, rel_bias, tq):
    m, d = q.shape
    b = m // tq
    n_pages = page_table.shape[1]
    _, page, nh, dh = cache_k.shape
    ppb = MOBA_BLOCK // page
    rows = nh * tq
    past = n_pages * page
    assert MOBA_BLOCK % page == 0 and past % MOBA_BLOCK == 0 and tq <= MOBA_BLOCK and page + 1 >= T5_SATURATION

    q_hq = q.reshape(b, tq, nh, dh).transpose(0, 2, 1, 3)
    ksum = cache_block_ksum(cache_k, page_table).transpose(0, 2, 1, 3)
    nblk = past // MOBA_BLOCK
    sel = sample_select(q_hq, ksum).reshape(b, nblk, 1, rows)
    bias_tiles, bias_new = sample_bias(rel_bias, tq, page)
    q_rows = q_hq.reshape(b, rows, dh)
    kn = k_new.reshape(b, rows, dh)
    vn = v_new.reshape(b, rows, dh)

    cols = page * nh
    n_phys = cache_k.shape[0]
    ck = cache_k.reshape(n_phys, cols, dh)
    cv = cache_v.reshape(n_phys, cols, dh)
    small = pl.BlockSpec((1, rows, dh), lambda bi, p, pt: (bi, 0, 0))
    pages = _page_specs(ppb, (cols, dh))
    far_tile = pl.BlockSpec((1, rows, cols), lambda bi, p, pt: (0, 0, 0))
    end_tile = pl.BlockSpec((1, rows, cols), lambda bi, p, pt: (jnp.where(p == nblk - 1, 1, 0), 0, 0))
    grid_spec = pltpu.PrefetchScalarGridSpec(
        num_scalar_prefetch=1,
        grid=(b, nblk),
        in_specs=[small, small, small] + pages + pages + [far_tile] * (ppb - 1) + [end_tile]
        + [pl.BlockSpec((1, 1, 1, rows), lambda bi, p, pt: (bi, p, 0, 0)),
           pl.BlockSpec((rows, rows), lambda bi, p, pt: (0, 0))],
        out_specs=small,
        scratch_shapes=[pltpu.VMEM((rows, 1), F32), pltpu.VMEM((rows, 1), F32), pltpu.VMEM((rows, dh), F32)],
    )
    o = pl.pallas_call(
        functools.partial(_moba_sample_kernel, ppb=ppb),
        grid_spec=grid_spec,
        out_shape=jax.ShapeDtypeStruct((b, rows, dh), F32),
        compiler_params=_params("parallel", "arbitrary"),
        name="moba_sample",
    )(page_table, q_rows, kn, vn, *([ck] * ppb), *([cv] * ppb), *([bias_tiles] * ppb), sel, bias_new)
    return o.reshape(b, nh, tq, dh).transpose(0, 2, 1, 3).reshape(m, d).astype(BF16)


def rwkv_time_mix(x, seq_len, shift0, wkv0, gain, w):
    m, d = x.shape
    (xr, xw, xk, xv, xa, xg), shift_new = rwkv_mix(x, gain, w["mu"], shift0, seq_len)
    r = matmul(xr, w["w_r"])
    k = matmul(xk, w["w_k"])
    v = matmul(xv, w["w_v"])
    wl = lora(xw, w["w1"], w["w2"], "tanh")
    al = lora(xa, w["a1"], w["a2"], "none")
    g = lora(xg, w["g1"], w["g2"], "sigmoid")
    y, h_fin = wkv(r, k, v, wl, al, g, w["prm"], jnp.swapaxes(wkv0, -1, -2), seq_len)
    x_new = matmul(y, w["w_o"], res=x)
    return x_new, jnp.swapaxes(h_fin, -1, -2), shift_new


def _rwkv_weights(i, rwkv_mu, rwkv_w_r, rwkv_w_k, rwkv_w_v, rwkv_w_o, rwkv_w0, rwkv_w1, rwkv_w2, rwkv_a0, rwkv_a1,
                  rwkv_a2, rwkv_g1, rwkv_g2, rwkv_k_k, rwkv_k_a, rwkv_r_k, rwkv_gn_g, rwkv_gn_b):
    d = rwkv_mu.shape[-1]
    w1, w2 = _pad_rank(rwkv_w1[i], rwkv_w2[i])
    a1, a2 = _pad_rank(rwkv_a1[i], rwkv_a2[i])
    g1, g2 = _pad_rank(rwkv_g1[i], rwkv_g2[i])
    prm = jnp.stack([rwkv_w0[i], rwkv_a0[i], rwkv_k_k[i], rwkv_k_a[i], rwkv_r_k[i].reshape(d), rwkv_gn_g[i],
                     rwkv_gn_b[i], jnp.zeros((d,), F32)]).astype(F32)
    return dict(mu=rwkv_mu[i], w_r=rwkv_w_r[i].astype(BF16), w_k=rwkv_w_k[i].astype(BF16),
                w_v=rwkv_w_v[i].astype(BF16), w_o=rwkv_w_o[i].astype(BF16),
                w1=w1, w2=w2, a1=a1, a2=a2, g1=g1, g2=g2, prm=prm)


def kernel(x_prompt, x_sample, state_wkv, state_shift, cache_k, cache_v, page_table, norm_mix, norm_ffn, norm_kv, norm_final, rwkv_mu, rwkv_w_r, rwkv_w_k, rwkv_w_v, rwkv_w_o, rwkv_w0, rwkv_w1, rwkv_w2, rwkv_a0, rwkv_a1, rwkv_a2, rwkv_g1, rwkv_g2, rwkv_k_k, rwkv_k_a, rwkv_r_k, rwkv_gn_g, rwkv_gn_b, kv_w, attn_w_q, attn_w_o, rel_bias, ffn_w_gate, ffn_w_up, ffn_w_down, moe_w_router, moe_b_router, moe_w_gate, moe_w_up, moe_w_down):
    d = x_prompt.shape[-1]
    nh = d // ATTN_HEAD_DIM
    assert norm_mix.shape[0] == 2 and rwkv_mu.shape[0] == 1 and attn_w_q.shape[0] == 1
    assert ffn_w_gate.shape[0] == 1 and moe_w_gate.shape[0] == 1

    w_rwkv = _rwkv_weights(0, rwkv_mu, rwkv_w_r, rwkv_w_k, rwkv_w_v, rwkv_w_o, rwkv_w0, rwkv_w1, rwkv_w2, rwkv_a0,
                           rwkv_a1, rwkv_a2, rwkv_g1, rwkv_g2, rwkv_k_k, rwkv_k_a, rwkv_r_k, rwkv_gn_g, rwkv_gn_b)
    wg, wu, wd = ffn_w_gate[0].astype(BF16), ffn_w_up[0].astype(BF16), ffn_w_down[0].astype(BF16)
    kvw_k, kvw_v = kv_w[:, :d].astype(BF16), kv_w[:, d:].astype(BF16)
    wq, wo = attn_w_q[0].astype(BF16), attn_w_o[0].astype(BF16)
    eg, eu, ed = moe_w_gate[0].astype(BF16), moe_w_up[0].astype(BF16), moe_w_down[0].astype(BF16)
    gains_kvq = jnp.stack([norm_kv, norm_mix[1]])

    def run_group(x3, wkv0, shift0, attend):
        b, t, _ = x3.shape
        x = x3.reshape(b * t, d)
        x, wkv_new, shift_new = rwkv_time_mix(x, t, shift0, wkv0, norm_mix[0], w_rwkv)
        x = ffn(x, norm_ffn[0], wg, wu, wd)
        h_kv, h_q = rmsnorm_multi(x, gains_kvq, [BF16, BF16])
        k, k_bf = matmul(h_kv, kvw_k, out_dtypes=(F32, BF16))
        v, v_bf = matmul(h_kv, kvw_v, out_dtypes=(F32, BF16))
        q = matmul(h_q, wq, out_dtypes=(BF16,))
        o = attend(q, k_bf, v_bf, t)
        x = matmul(o, wo, res=x)
        y = moe_layer(x, norm_ffn[1], norm_final, moe_w_router[0], moe_b_router[0], eg, eu, ed)
        kv_shape = (b, t, nh, ATTN_HEAD_DIM)
        return y.reshape(b, t, d), wkv_new[None], shift_new[None], k.reshape(kv_shape), v.reshape(kv_shape)

    bp = x_prompt.shape[0]
    wkv_zero = jnp.zeros((bp,) + state_wkv.shape[2:], state_wkv.dtype)
    shift_zero = jnp.zeros((bp, d), state_shift.dtype)
    out_p = run_group(
        x_prompt, wkv_zero, shift_zero,
        lambda q, k, v, t: moba_prompt(q, k, v, rel_bias, t))
    out_s = run_group(
        x_sample, state_wkv[0], state_shift[0],
        lambda q, k, v, t: moba_sample(q, k, v, cache_k, cache_v, page_table, rel_bias, t))
    return (out_p[0], out_s[0]) + out_p[1:] + out_s[1:]
```

```python
import functools
import math

import jax
import jax.numpy as jnp
from jax import lax
from jax.experimental import pallas as pl
from jax.experimental.pallas import tpu as pltpu

F32 = jnp.float32
BF16 = jnp.bfloat16

RWKV_HEAD = 64
ATTN_HEAD_DIM = 128
MOBA_BLOCK = 256
MOBA_TOPK = 3
REL_BUCKETS = 32
REL_MAX_DIST = 128
MOE_TOPK = 2
GN_EPS = 64e-5
RMS_EPS = 1e-6
NEG_INF = -1e30

V7X_VMEM_LIMIT = 48 * 1024 * 1024
LANES = 128

_HI = lax.Precision.HIGHEST


def _params(*sem):
    return pltpu.CompilerParams(dimension_semantics=sem, vmem_limit_bytes=V7X_VMEM_LIMIT)


def _dot(a, b):
    return jnp.dot(a, b, preferred_element_type=F32)


def _dot_hi(a, b):
    return jnp.dot(a, b, preferred_element_type=F32, precision=_HI)


def _dot_nt(a, b, precision=None):
    return lax.dot_general(a, b, (((1,), (1,)), ((), ())), preferred_element_type=F32, precision=precision)


def _pick(n, prefs):
    for p in prefs:
        if n % p == 0:
            return p
    return n


def _mm_kernel(x_ref, w_ref, *rest, has_res):
    res_ref = rest[0] if has_res else None
    o_refs = rest[1:] if has_res else rest
    acc = _dot(x_ref[...], w_ref[...])
    if has_res:
        acc = acc + res_ref[...]
    for o_ref in o_refs:
        o_ref[...] = acc.astype(o_ref.dtype)


def matmul(x, w, res=None, out_dtypes=(F32,)):
    m, k = x.shape
    n = w.shape[1]
    tm = _pick(m, (1024, 512, 256, 128, 64, 32, 16, 8))
    tn = _pick(n, (1024, 512, 256, 128))
    in_specs = [pl.BlockSpec((tm, k), lambda i, j: (i, 0)), pl.BlockSpec((k, tn), lambda i, j: (0, j))]
    args = [x, w]
    if res is not None:
        in_specs.append(pl.BlockSpec((tm, tn), lambda i, j: (i, j)))
        args.append(res)
    outs = pl.pallas_call(
        functools.partial(_mm_kernel, has_res=res is not None),
        grid=(m // tm, n // tn),
        in_specs=in_specs,
        out_specs=[pl.BlockSpec((tm, tn), lambda i, j: (i, j)) for _ in out_dtypes],
        out_shape=[jax.ShapeDtypeStruct((m, n), dt) for dt in out_dtypes],
        compiler_params=_params("parallel", "arbitrary"),
        name="matmul",
    )(*args)
    return outs[0] if len(out_dtypes) == 1 else outs


def _rms(xf):
    return xf * lax.rsqrt(jnp.mean(xf * xf, axis=-1, keepdims=True) + RMS_EPS)


def _norm_kernel(x_ref, g_ref, *o_refs):
    y = _rms(x_ref[...])
    for j, o_ref in enumerate(o_refs):
        o_ref[...] = (y * g_ref[j:j + 1, :]).astype(o_ref.dtype)


def rmsnorm_multi(x, gains, out_dtypes):
    m, d = x.shape
    n = gains.shape[0]
    tm = _pick(m, (512, 256, 128, 64, 32, 16, 8))
    return pl.pallas_call(
        _norm_kernel,
        grid=(m // tm,),
        in_specs=[pl.BlockSpec((tm, d), lambda i: (i, 0)), pl.BlockSpec((n, d), lambda i: (0, 0))],
        out_specs=[pl.BlockSpec((tm, d), lambda i: (i, 0)) for _ in range(n)],
        out_shape=[jax.ShapeDtypeStruct((m, d), dt) for dt in out_dtypes],
        compiler_params=_params("parallel"),
        name="rmsnorm",
    )(x, gains)


def _mix_kernel(x_ref, g_ref, mu_ref, s0_ref, *rest, blocks_per_seq):
    o_refs, last_ref, carry_ref = rest[:6], rest[6], rest[7]
    i = pl.program_id(0)
    h = _rms(x_ref[...]) * g_ref[...]
    tm = h.shape[0]

    @pl.when(i % blocks_per_seq == 0)
    def _():
        carry_ref[...] = s0_ref[0]

    row = lax.broadcasted_iota(jnp.int32, h.shape, 0)
    prev = jnp.where(row == 0, carry_ref[...], pltpu.roll(h, 1, 0))
    xx = prev - h
    for j in range(6):
        o_refs[j][...] = (h + xx * mu_ref[j:j + 1, :]).astype(o_refs[j].dtype)
    last = h[tm - 1:tm, :]
    carry_ref[...] = last
    last_ref[0] = last


def rwkv_mix(x, gain, mu, shift0, seq_len):
    m, d = x.shape
    b = m // seq_len
    tm = _pick(seq_len, (256, 128, 64, 32, 16, 8))
    bps = seq_len // tm
    outs = pl.pallas_call(
        functools.partial(_mix_kernel, blocks_per_seq=bps),
        grid=(m // tm,),
        in_specs=[
            pl.BlockSpec((tm, d), lambda i: (i, 0)),
            pl.BlockSpec((1, d), lambda i: (0, 0)),
            pl.BlockSpec((6, d), lambda i: (0, 0)),
            pl.BlockSpec((1, 1, d), lambda i: (i // bps, 0, 0)),
        ],
        out_specs=[pl.BlockSpec((tm, d), lambda i: (i, 0)) for _ in range(6)]
        + [pl.BlockSpec((1, 1, d), lambda i: (i // bps, 0, 0))],
        out_shape=[jax.ShapeDtypeStruct((m, d), BF16) for _ in range(6)]
        + [jax.ShapeDtypeStruct((b, 1, d), F32)],
        scratch_shapes=[pltpu.VMEM((1, d), F32)],
        compiler_params=_params("arbitrary"),
        name="rwkv_mix",
    )(x, gain.reshape(1, d), mu, shift0.reshape(b, 1, d))
    return outs[:6], outs[6].reshape(b, d)


def _lora_kernel(x_ref, w1_ref, w2_ref, o_ref, *, act):
    t = _dot(x_ref[...], w1_ref[...])
    if act == "tanh":
        t = jnp.tanh(t)
    elif act == "sigmoid":
        t = 1.0 / (1.0 + jnp.exp(-t))
    o_ref[...] = _dot(t.astype(BF16), w2_ref[...])


def lora(x, w1, w2, act):
    m, d = x.shape
    r = w1.shape[1]
    n = w2.shape[1]
    tm = _pick(m, (512, 256, 128, 64, 32, 16, 8))
    return pl.pallas_call(
        functools.partial(_lora_kernel, act=act),
        grid=(m // tm,),
        in_specs=[
            pl.BlockSpec((tm, d), lambda i: (i, 0)),
            pl.BlockSpec((d, r), lambda i: (0, 0)),
            pl.BlockSpec((r, n), lambda i: (0, 0)),
        ],
        out_specs=pl.BlockSpec((tm, n), lambda i: (i, 0)),
        out_shape=jax.ShapeDtypeStruct((m, n), F32),
        compiler_params=_params("parallel"),
        name="lora",
    )(x, w1, w2)


def _pad_rank(w1, w2):
    r = w1.shape[1]
    rp = -(-r // LANES) * LANES
    if rp != r:
        w1 = jnp.pad(w1, ((0, 0), (0, rp - r)))
        w2 = jnp.pad(w2, ((0, rp - r), (0, 0)))
    return w1.astype(BF16), w2.astype(BF16)


def _split(a):
    hi = a.astype(BF16)
    return hi, (a - hi.astype(F32)).astype(BF16)


def _dot3(a, b, mode):
    (ah, al), (bh, bl) = a, b
    la = 0 if mode == "tn" else 1
    lb = 1 if mode == "nt" else 0
    lhs = jnp.concatenate([ah, ah, al], axis=la)
    rhs = jnp.concatenate([bh, bl, bh], axis=lb)
    return lax.dot_general(lhs, rhs, (((la,), (lb,)), ((), ())), preferred_element_type=F32)


def _wkv_kernel(r_ref, k_ref, v_ref, wl_ref, al_ref, g_ref, prm_ref, h0_ref, y_ref, hout_ref, h_scr,
                *, chunk, pairs):
    c_idx = pl.program_id(2)
    n = RWKV_HEAD
    pw = 2 * n
    cw = 2 * chunk
    tb = r_ref.shape[0]
    n_chunks = tb // chunk

    def iota(shape, dim):
        return lax.broadcasted_iota(jnp.int32, shape, dim)

    def block_diag(x, w):
        even = iota((1, 2 * w), 1) < w
        return jnp.concatenate([jnp.where(even, x, 0.0), jnp.where(even, 0.0, x)], axis=0)

    def split_bd(x, w):
        even = jnp.where(iota((1, 2 * w), 1) < w, 1.0, 0.0).astype(BF16)
        hi, lo = _split(x)
        return tuple(jnp.concatenate([t * even, t * (1 - even)], axis=0) for t in (hi, lo))

    def cat_rows(a, b):
        return tuple(jnp.concatenate([s, t], axis=0) for s, t in zip(a, b))

    @pl.when(c_idx == 0)
    def _():
        for p in range(pairs):
            h_scr[p] = block_diag(jnp.concatenate([h0_ref[0, 2 * p], h0_ref[0, 2 * p + 1]], axis=1), n)

    ri = iota((chunk, cw), 0)
    ci = iota((chunk, cw), 1) % chunk
    strict = ri > ci
    incl = ri >= ci
    eye_c = jnp.where(ri == ci, 1.0, 0.0)
    tri = jnp.where(iota((chunk, chunk), 0) >= iota((chunk, chunk), 1), 1.0, 0.0).astype(BF16)
    tri3 = jnp.concatenate([tri, tri, tri], axis=1)
    same_head = iota((pw, pw), 0) // n == iota((pw, pw), 1) // n
    seg = jnp.where(same_head, 1.0, 0.0).astype(BF16)
    seg2 = jnp.concatenate([seg, seg], axis=0)
    eye_pw = jnp.where(iota((pw, pw), 0) == iota((pw, pw), 1), 1.0, 0.0)
    n_double = max(int(math.log2(chunk)) - 1, 0)

    def head_sum(x):
        x_h, x_l = _split(x)
        return _dot(jnp.concatenate([x_h, x_l], axis=1), seg2)

    def one_chunk(ci_, hs):
        rows = pl.ds(pl.multiple_of(ci_ * chunk, chunk), chunk)
        r_all = r_ref[rows, :]
        k_all = k_ref[rows, :]
        v_all = v_ref[rows, :]
        wl_all = wl_ref[rows, :]
        al_all = al_ref[rows, :]
        pr = range(pairs)
        sls = [slice(j * pw, (j + 1) * pw) for j in pr]
        prm = lambda row, j: prm_ref[row:row + 1, sls[j]]
        r = [r_all[:, sl] for sl in sls]
        k = [k_all[:, sl] for sl in sls]
        v = [v_all[:, sl] for sl in sls]
        z = [-(prm(0, j) + wl_all[:, sls[j]]) for j in pr]
        ld = [-jnp.exp(-(jnp.maximum(zj, 0.0) + jnp.log(1.0 + jnp.exp(-jnp.abs(zj)))) - 0.5) for zj in z]
        alpha = [1.0 / (1.0 + jnp.exp(-(prm(1, j) + al_all[:, sls[j]]))) for j in pr]
        kkv = [k[j] * prm(2, j) for j in pr]
        kk = [kkv[j] / jnp.maximum(jnp.sqrt(head_sum(kkv[j] * kkv[j])), 1e-12) for j in pr]
        k2 = [k[j] * (1.0 + (alpha[j] - 1.0) * prm(3, j)) for j in pr]

        def ld3(x):
            x_h, x_m = _split(x)
            return jnp.concatenate([x_h, x_m, (x - x_h.astype(F32) - x_m.astype(F32)).astype(BF16)], axis=0)

        cum = [_dot(tri3, ld3(ld[j])) for j in pr]
        w_in = [jnp.exp(c) for c in cum]
        w_inv = [jnp.exp(-c) for c in cum]
        w_tot = [w[chunk - 1:chunk, :] for w in w_in]
        ar = [_split(jnp.concatenate([-kk[j] * jnp.exp(cum[j] - ld[j]), r[j] * w_in[j]], axis=0)) for j in pr]
        b_t = [kk[j] * alpha[j] * w_inv[j] for j in pr]
        k_t = [k2[j] * w_inv[j] for j in pr]
        bk_bd = [cat_rows(split_bd(b_t[j], n), split_bd(k_t[j], n)) for j in pr]
        big = [_dot3(ar[j], bk_bd[j], "nt") for j in pr]
        a_ab = [jnp.where(strict, b_[:chunk, :cw], 0.0) for b_ in big]
        a_ak = [jnp.where(strict, b_[:chunk, cw:], 0.0) for b_ in big]
        a_r = [jnp.concatenate([jnp.where(incl, b_[chunk:, :cw], 0.0),
                                jnp.where(incl, b_[chunk:, cw:], 0.0)], axis=1) for b_ in big]
        tinv = [eye_c + a for a in a_ab]
        p = a_ab
        for _ in range(n_double):
            p = [_dot3(_split(x), split_bd(x, chunk), "nn") for x in p]
            tinv = [tinv[j] + _dot3(_split(tinv[j]), split_bd(p[j], chunk), "nn") for j in pr]

        hdep = [_dot3(ar[j], _split(hs[j]), "nn") for j in pr]
        v_bd = [split_bd(v[j], n) for j in pr]
        akv = [_dot3(_split(a_ak[j]), v_bd[j], "nn") for j in pr]
        u = [_dot3(_split(tinv[j]), split_bd(hdep[j][:chunk] + akv[j], n), "nn") for j in pr]
        uv_bd = [cat_rows(split_bd(u[j], n), v_bd[j]) for j in pr]
        o = [hdep[j][chunk:] + _dot3(_split(a_r[j]), uv_bd[j], "nn") for j in pr]
        w_col = [jnp.sum(eye_pw * w_tot[j], axis=1, keepdims=True) for j in pr]
        bk_s = [_split(jnp.concatenate([b_t[j], k_t[j]], axis=0) * w_tot[j]) for j in pr]
        uv = [_split(jnp.concatenate([u[j], v[j]], axis=0)) for j in pr]
        h_new = [hs[j] * w_col[j] + jnp.where(same_head, _dot3(bk_s[j], uv[j], "tn"), 0.0) for j in pr]

        outs = []
        for j in pr:
            d_ = o[j] - head_sum(o[j]) * (1.0 / n)
            var = head_sum(d_ * d_) * (1.0 / n)
            on = d_ * lax.rsqrt(var + GN_EPS) * prm(5, j) + prm(6, j)
            outs.append(on + head_sum(r[j] * k2[j] * prm(4, j)) * v[j])
        y_ref[rows, :] = (jnp.concatenate(outs, axis=1) * g_ref[rows, :]).astype(y_ref.dtype)
        return tuple(h_new)

    hs = lax.fori_loop(0, n_chunks, one_chunk, tuple(h_scr[j] for j in range(pairs)))
    for j in range(pairs):
        h_scr[j] = hs[j]

    @pl.when(c_idx == pl.num_programs(2) - 1)
    def _():
        for j in range(pairs):
            hout_ref[0, 2 * j] = hs[j][:n, :n]
            hout_ref[0, 2 * j + 1] = hs[j][n:, n:]


def wkv(r, k, v, wl, al, g, prm, h0, seq_len):
    m, d = r.shape
    b = m // seq_len
    n = RWKV_HEAD
    nh = d // n
    heads = _pick(nh, (16, 8, 4, 2))
    assert nh % heads == 0 and heads % 2 == 0
    chunk = min(64, seq_len)
    tb = _pick(seq_len, (512, 256, 128, 64, 32, 16, 8))
    nb = seq_len // tb
    wd = heads * n
    row_spec = pl.BlockSpec((tb, wd), lambda bi, hi, ci: (bi * nb + ci, hi))
    st_spec = pl.BlockSpec((1, heads, n, n), lambda bi, hi, ci: (bi, hi, 0, 0))
    return pl.pallas_call(
        functools.partial(_wkv_kernel, chunk=chunk, pairs=heads // 2),
        grid=(b, nh // heads, nb),
        in_specs=[row_spec] * 6 + [pl.BlockSpec((8, wd), lambda bi, hi, ci: (0, hi)), st_spec],
        out_specs=[row_spec, st_spec],
        out_shape=[jax.ShapeDtypeStruct((m, d), BF16), jax.ShapeDtypeStruct((b, nh, n, n), F32)],
        scratch_shapes=[pltpu.VMEM((heads // 2, 2 * n, 2 * n), F32)],
        compiler_params=_params("parallel", "parallel", "arbitrary"),
        name="wkv",
    )(r, k, v, wl, al, g, prm, h0)


def _silu_mul(g, u):
    return g * (1.0 / (1.0 + jnp.exp(-g))) * u


def _ffn_kernel(x_ref, g_ref, wg_ref, wu_ref, wd_ref, o_ref, h_ref, acc_ref):
    j = pl.program_id(1)

    @pl.when(j == 0)
    def _():
        x = x_ref[...]
        acc_ref[...] = x
        h_ref[...] = (_rms(x) * g_ref[...]).astype(BF16)

    h = h_ref[...]
    a = _silu_mul(_dot(h, wg_ref[...]), _dot(h, wu_ref[...])).astype(BF16)
    acc_ref[...] += _dot(a, wd_ref[...])

    @pl.when(j == pl.num_programs(1) - 1)
    def _():
        o_ref[...] = acc_ref[...]


def ffn(x, gain, wg, wu, wd):
    m, d = x.shape
    f = wg.shape[1]
    tm = _pick(m, (512, 256, 128, 64, 32, 16, 8))
    tf = _pick(f, (512, 256, 128))
    return pl.pallas_call(
        _ffn_kernel,
        grid=(m // tm, f // tf),
        in_specs=[
            pl.BlockSpec((tm, d), lambda i, j: (i, 0)),
            pl.BlockSpec((1, d), lambda i, j: (0, 0)),
            pl.BlockSpec((d, tf), lambda i, j: (0, j)),
            pl.BlockSpec((d, tf), lambda i, j: (0, j)),
            pl.BlockSpec((tf, d), lambda i, j: (j, 0)),
        ],
        out_specs=pl.BlockSpec((tm, d), lambda i, j: (i, 0)),
        out_shape=jax.ShapeDtypeStruct((m, d), F32),
        scratch_shapes=[pltpu.VMEM((tm, d), BF16), pltpu.VMEM((tm, d), F32)],
        compiler_params=_params("parallel", "arbitrary"),
        name="ffn",
    )(x, gain.reshape(1, d), wg, wu, wd)


def _moe_ffn_kernel(te_ref, tv_ref, src_ref, x_hbm, wg_ref, wu_ref, wd_ref, o_ref, xbuf, acc_ref, sems,
                    *, per_step, n_steps):
    i = pl.program_id(0)
    j = pl.program_id(1)
    nt = pl.num_programs(0)
    tm = o_ref.shape[0]
    n_req = per_step * n_steps
    slot = i % 2

    def request(tile, row, s):
        src_row = src_ref[tile * tm + jnp.minimum(row, tm - 1)]
        return _row_copy(x_hbm, xbuf.at[s], sems.at[s], src_row, row)

    def wait_tile(s):
        for r in range(n_req):
            _row_copy(x_hbm, xbuf.at[s], sems.at[s], 0, r).wait()

    @pl.when((i == 0) & (j == 0))
    def _():
        def start(r, c):
            request(0, r, 0).start()
            return c
        lax.fori_loop(0, n_req, start, 0)

    def request_next():
        nxt = jnp.minimum(i + 1, nt - 1)
        for r in range(per_step):
            request(nxt, j * per_step + r, 1 - slot).start()

    @pl.when(j == 0)
    def _():
        wait_tile(slot)

    @pl.when(tv_ref[i] > 0)
    def _():
        @pl.when(j == 0)
        def _():
            acc_ref[...] = jnp.zeros_like(acc_ref)

        request_next()
        x = xbuf[slot, 0:tm, :].astype(BF16)
        a = _silu_mul(_dot(x, wg_ref[0]), _dot(x, wu_ref[0])).astype(BF16)
        acc_ref[...] += _dot(a, wd_ref[0])

        @pl.when(j == pl.num_programs(1) - 1)
        def _():
            o_ref[...] = acc_ref[...]

    @pl.when(tv_ref[i] == 0)
    def _():
        request_next()
        o_ref[...] = jnp.zeros_like(o_ref)

    @pl.when((i == nt - 1) & (j == n_steps - 1))
    def _():
        wait_tile(1 - slot)


def moe_ffn(x, src_token, wg, wu, wd, tile_expert, tile_valid, tm):
    d = x.shape[1]
    mp = src_token.shape[0]
    f = wg.shape[2]
    tf = _pick(f, (512, 256, 128))
    nf = f // tf
    per_step = -(-tm // nf)
    tm_buf = -(-per_step * nf // 8) * 8

    def fj(i, j, tv):
        return jnp.where(tv[i] > 0, j, nf - 1)

    grid_spec = pltpu.PrefetchScalarGridSpec(
        num_scalar_prefetch=3,
        grid=(mp // tm, nf),
        in_specs=[
            pl.BlockSpec(memory_space=pl.ANY),
            pl.BlockSpec((1, d, tf), lambda i, j, te, tv, src: (te[i], 0, fj(i, j, tv))),
            pl.BlockSpec((1, d, tf), lambda i, j, te, tv, src: (te[i], 0, fj(i, j, tv))),
            pl.BlockSpec((1, tf, d), lambda i, j, te, tv, src: (te[i], fj(i, j, tv), 0)),
        ],
        out_specs=pl.BlockSpec((tm, d), lambda i, j, te, tv, src: (i, 0)),
        scratch_shapes=[pltpu.VMEM((2, tm_buf, d), F32), pltpu.VMEM((tm, d), F32), pltpu.SemaphoreType.DMA((2,))],
    )
    return pl.pallas_call(
        functools.partial(_moe_ffn_kernel, per_step=per_step, n_steps=nf),
        grid_spec=grid_spec,
        out_shape=jax.ShapeDtypeStruct((mp, d), F32),
        compiler_params=_params("arbitrary", "arbitrary"),
        name="moe_ffn",
    )(tile_expert, tile_valid, src_token, x, wg, wu, wd)


def _router_kernel(x_ref, g_ref, wr_ref, br_ref, h_ref, rt_ref):
    h = _rms(x_ref[...]) * g_ref[...]
    h_ref[...] = h
    logits = _dot_hi(h, wr_ref[...]) + br_ref[...]
    lane = lax.broadcasted_iota(jnp.int32, logits.shape, 1)
    m1 = jnp.max(logits, axis=-1, keepdims=True)
    i1 = jnp.min(jnp.where(logits == m1, lane, LANES), axis=-1, keepdims=True)
    rest = jnp.where(lane == i1, -jnp.inf, logits)
    m2 = jnp.max(rest, axis=-1, keepdims=True)
    i2 = jnp.min(jnp.where(rest == m2, lane, LANES), axis=-1, keepdims=True)
    e = jnp.exp(m2 - m1)
    g1 = 1.0 / (1.0 + e)
    g2 = e / (1.0 + e)
    rt_ref[...] = jnp.where(lane == 0, g1, jnp.where(lane == 1, g2, jnp.where(
        lane == 2, i1.astype(F32), jnp.where(lane == 3, i2.astype(F32), 0.0))))


def router(x, gain, w_router, b_router):
    m, d = x.shape
    e = w_router.shape[1]
    wr = jnp.pad(w_router.astype(F32), ((0, 0), (0, LANES - e)))
    br = jnp.pad(b_router.astype(F32), (0, LANES - e), constant_values=NEG_INF).reshape(1, LANES)
    tm = _pick(m, (512, 256, 128, 64, 32, 16, 8))
    return pl.pallas_call(
        _router_kernel,
        grid=(m // tm,),
        in_specs=[
            pl.BlockSpec((tm, d), lambda i: (i, 0)),
            pl.BlockSpec((1, d), lambda i: (0, 0)),
            pl.BlockSpec((d, LANES), lambda i: (0, 0)),
            pl.BlockSpec((1, LANES), lambda i: (0, 0)),
        ],
        out_specs=[pl.BlockSpec((tm, d), lambda i: (i, 0)), pl.BlockSpec((tm, LANES), lambda i: (i, 0))],
        out_shape=[jax.ShapeDtypeStruct((m, d), F32), jax.ShapeDtypeStruct((m, LANES), F32)],
        compiler_params=_params("parallel"),
        name="router",
    )(x, gain.reshape(1, d), wr, br)


def _row_copy(src_hbm, dst_ref, sem, src_row, dst_row):
    return pltpu.make_async_copy(src_hbm.at[pl.ds(src_row, 1)], dst_ref.at[pl.ds(dst_row, 1)], sem)


def _combine_kernel(p1_ref, p2_ref, y_hbm, x_ref, rt_ref, g_ref, o_ref, b1_ref, b2_ref, sem1, sem2):
    t = pl.program_id(0)
    tr = x_ref.shape[0]
    slot = t % 2

    def request(tile, s):
        for r in range(tr):
            _row_copy(y_hbm, b1_ref.at[s], sem1.at[s], p1_ref[tile * tr + r], r).start()
            _row_copy(y_hbm, b2_ref.at[s], sem2.at[s], p2_ref[tile * tr + r], r).start()

    @pl.when(t == 0)
    def _():
        request(0, 0)

    @pl.when(t + 1 < pl.num_programs(0))
    def _():
        request(t + 1, 1 - slot)

    for r in range(tr):
        _row_copy(y_hbm, b1_ref.at[slot], sem1.at[slot], 0, r).wait()
        _row_copy(y_hbm, b2_ref.at[slot], sem2.at[slot], 0, r).wait()
    rt = rt_ref[...]
    x = x_ref[...] + rt[:, 0:1] * b1_ref[slot] + rt[:, 1:2] * b2_ref[slot]
    o_ref[...] = _rms(x) * g_ref[...]


def moe_combine(x, y_sorted, rt, pos1, pos2, gain):
    m, d = x.shape
    tr = _pick(m, (128, 64, 32, 16, 8))
    grid_spec = pltpu.PrefetchScalarGridSpec(
        num_scalar_prefetch=2,
        grid=(m // tr,),
        in_specs=[
            pl.BlockSpec(memory_space=pl.ANY),
            pl.BlockSpec((tr, d), lambda i, p1, p2: (i, 0)),
            pl.BlockSpec((tr, LANES), lambda i, p1, p2: (i, 0)),
            pl.BlockSpec((1, d), lambda i, p1, p2: (0, 0)),
        ],
        out_specs=pl.BlockSpec((tr, d), lambda i, p1, p2: (i, 0)),
        scratch_shapes=[pltpu.VMEM((2, tr, d), F32), pltpu.VMEM((2, tr, d), F32),
                        pltpu.SemaphoreType.DMA((2,)), pltpu.SemaphoreType.DMA((2,))],
    )
    return pl.pallas_call(
        _combine_kernel,
        grid_spec=grid_spec,
        out_shape=jax.ShapeDtypeStruct((m, d), F32),
        compiler_params=_params("arbitrary"),
        name="moe_combine",
    )(pos1, pos2, y_sorted, x, rt, gain.reshape(1, d))


def moe_layer(x, gain_ffn, gain_final, w_router, b_router, wg, wu, wd):
    m, d = x.shape
    n_exp = wg.shape[0]
    h, rt = router(x, gain_ffn, w_router, b_router)
    tm = 512 if m * MOE_TOPK >= 8 * 512 else 128
    experts = rt[:, 2:4].astype(jnp.int32).reshape(-1)
    n_asg = experts.shape[0]
    counts = jnp.sum(experts[:, None] == jnp.arange(n_exp)[None, :], axis=0)
    padded = ((counts + tm - 1) // tm) * tm
    start = jnp.cumsum(counts) - counts
    pstart = jnp.cumsum(padded) - padded
    order = jnp.argsort(experts, stable=True)
    rank = jnp.zeros((n_asg,), jnp.int32).at[order].set(jnp.arange(n_asg, dtype=jnp.int32))
    dest = (pstart[experts] + rank - start[experts]).astype(jnp.int32)
    mp = n_asg + n_exp * tm
    src_token = jnp.zeros((mp,), jnp.int32).at[dest].set(jnp.arange(n_asg, dtype=jnp.int32) // MOE_TOPK)
    tile_start = jnp.arange(mp // tm, dtype=jnp.int32) * tm
    pend = jnp.cumsum(padded)
    tile_expert = jnp.minimum(jnp.sum(tile_start[:, None] >= pend[None, :], axis=1), n_exp - 1).astype(jnp.int32)
    tile_valid = (tile_start < pend[-1]).astype(jnp.int32)
    last_used = jnp.max(jnp.where(tile_valid > 0, tile_expert, 0))
    tile_expert = jnp.where(tile_valid > 0, tile_expert, last_used)

    y_sorted = moe_ffn(h, src_token, wg, wu, wd, tile_expert, tile_valid, tm)
    pos = dest.reshape(m, MOE_TOPK)
    return moe_combine(x, y_sorted, rt, pos[:, 0], pos[:, 1], gain_final)


_T5_EXACT = REL_BUCKETS // 2
_T5_STEPS = [_T5_EXACT] + [
    math.ceil(_T5_EXACT * (REL_MAX_DIST / _T5_EXACT) ** (n / (REL_BUCKETS - _T5_EXACT)))
    for n in range(1, REL_BUCKETS - _T5_EXACT)]
T5_SATURATION = _T5_STEPS[-1]


def _rel_bias_of_dist(d, table):
    val = jnp.where(d >= 1, table(1), table(0))
    for b in range(2, _T5_EXACT):
        val = jnp.where(d >= b, table(b), val)
    for n, t in enumerate(_T5_STEPS):
        val = jnp.where(d >= t, table(_T5_EXACT + n), val)
    return val


def _prompt_bias_kernel(rb_ref, dg_ref, aj_ref):
    h = pl.program_id(0)
    blk = dg_ref.shape[1]
    ri = lax.broadcasted_iota(jnp.int32, (blk, blk), 0)
    ci = lax.broadcasted_iota(jnp.int32, (blk, blk), 1)
    table = lambda b: rb_ref[h, b]
    far = table(REL_BUCKETS - 1)
    dg_ref[0] = jnp.where(ri >= ci, _rel_bias_of_dist(ri - ci, table) - far, NEG_INF)
    aj_ref[0] = _rel_bias_of_dist(ri - ci + blk, table) - far


def prompt_bias(rel_bias):
    nh = rel_bias.shape[1]
    blk = MOBA_BLOCK
    tile = pl.BlockSpec((1, blk, blk), lambda h: (h, 0, 0))
    return pl.pallas_call(
        _prompt_bias_kernel,
        grid=(nh,),
        in_specs=[pl.BlockSpec(memory_space=pltpu.SMEM)],
        out_specs=[tile, tile],
        out_shape=[jax.ShapeDtypeStruct((nh, blk, blk), F32)] * 2,
        compiler_params=_params("arbitrary"),
        name="prompt_bias",
    )(rel_bias.T)


_GATE_ROWS = 16


def _moba_prompt_kernel(c_last_ref, q_ref, k_ref, v_ref, dg_ref, aj_ref, o_ref, s_ref, ks_ref):
    h = pl.program_id(1)
    qi = pl.program_id(2)
    blk = MOBA_BLOCK
    nblk = k_ref.shape[0] // blk
    scale = ATTN_HEAD_DIM ** -0.5
    q = q_ref[...]
    c_last = c_last_ref[h]

    @pl.when(qi == 0)
    def _():
        ks_ref[...] = jnp.zeros_like(ks_ref)
        for j in range(nblk):
            ks_ref[j:j + 1, :] = jnp.sum(k_ref[j * blk:(j + 1) * blk, :].astype(F32), axis=0, keepdims=True)

    half_rows = _GATE_ROWS // 2
    ks_hi, ks_lo = _split(ks_ref[...])
    gate = (_dot_nt(ks_hi, q) + _dot_nt(ks_lo, q))[:half_rows]
    row = lax.broadcasted_iota(jnp.int32, gate.shape, 0)
    gate = jnp.where(row < qi, gate, -jnp.inf)
    cnt = jnp.zeros(gate.shape, jnp.int32)
    for n in range(nblk):
        g_n = gate[n:n + 1, :]
        cnt = cnt + jnp.where((g_n > gate) | ((g_n == gate) & (n < row)), 1, 0)
    attend = ((cnt < MOBA_TOPK) & (row < qi)) | (row == qi)
    c_full = jnp.full(gate.shape, c_last, F32)
    c_hi = c_full.astype(BF16).astype(F32)
    mask_t = jnp.concatenate([jnp.where(attend, c_hi, NEG_INF), jnp.where(attend, c_full - c_hi, 0.0)],
                             axis=0).astype(BF16)

    def attend_blocks(nk):
        tk = nk * blk
        kb = lax.broadcasted_iota(jnp.int32, (_GATE_ROWS, tk), 1) // blk
        rb = lax.broadcasted_iota(jnp.int32, (_GATE_ROWS, tk), 0) % half_rows
        expand = jnp.where(kb == rb, 1.0, 0.0).astype(BF16)
        mask = lax.dot_general(mask_t, expand, (((0,), (0,)), ((), ())), preferred_element_type=F32)
        lm = _dot_nt(q, k_ref[0:tk, :]) * scale + mask
        for j in range(nk):
            s_ref[j] = lm[:, j * blk:(j + 1) * blk]
        s_ref[qi] += dg_ref[0]

        @pl.when(qi > 0)
        def _():
            s_ref[qi - 1] += aj_ref[0]

        m = jnp.max(s_ref[0], axis=-1, keepdims=True)
        for j in range(1, nk):
            m = jnp.maximum(m, jnp.max(s_ref[j], axis=-1, keepdims=True))
        p = [jnp.exp(s_ref[j] - m) for j in range(nk)]
        l = sum(jnp.sum(x, axis=-1, keepdims=True) for x in p)
        acc = _dot(jnp.concatenate([x.astype(BF16) for x in p], axis=1), v_ref[0:tk, :])
        o_ref[...] = (acc / l).astype(o_ref.dtype)

    lo = 0
    for nk in range(2, nblk + 2, 2):
        nk = min(nk, nblk)
        pl.when((qi >= lo) & (qi < nk))(functools.partial(attend_blocks, nk))
        lo = nk


def moba_prompt(q, k, v, rel_bias, seq_len):
    m, d = q.shape
    dh = ATTN_HEAD_DIM
    nh = d // dh
    b = m // seq_len
    blk = MOBA_BLOCK
    nq = seq_len // blk
    assert seq_len % blk == 0 and nq <= _GATE_ROWS // 2
    dg, aj = prompt_bias(rel_bias)
    kv_spec = pl.BlockSpec((seq_len, dh), lambda bi, hi, qi: (bi, hi))
    tile = pl.BlockSpec((1, blk, blk), lambda bi, hi, qi: (hi, 0, 0))
    q_spec = pl.BlockSpec((blk, dh), lambda bi, hi, qi: (bi * nq + qi, hi))
    return pl.pallas_call(
        _moba_prompt_kernel,
        grid=(b, nh, nq),
        in_specs=[pl.BlockSpec(memory_space=pltpu.SMEM), q_spec, kv_spec, kv_spec, tile, tile],
        out_specs=q_spec,
        out_shape=jax.ShapeDtypeStruct((m, d), BF16),
        scratch_shapes=[pltpu.VMEM((nq, blk, blk), F32), pltpu.VMEM((_GATE_ROWS, dh), F32)],
        compiler_params=_params("parallel", "parallel", "arbitrary"),
        name="moba_prompt",
    )(rel_bias[REL_BUCKETS - 1], q, k, v, dg, aj)


def _page_specs(n, page_shape):
    zeros = (0,) * len(page_shape)
    return [pl.BlockSpec((1,) + page_shape, lambda bi, p, pt, i=i: (pt[bi, n * p + i],) + zeros) for i in range(n)]


def _ksum_kernel(pt_ref, *refs, pages_per_block):
    k_refs, o_ref = refs[:-1], refs[-1]
    for blk in range(len(k_refs) // pages_per_block):
        s = jnp.sum(k_refs[blk * pages_per_block][0], axis=0)
        for i in range(1, pages_per_block):
            s = s + jnp.sum(k_refs[blk * pages_per_block + i][0], axis=0)
        o_ref[0, blk] = s


def cache_block_ksum(cache_k, page_table):
    b, n_pages = page_table.shape
    _, page, nh, dh = cache_k.shape
    ppb = MOBA_BLOCK // page
    pps = _pick(n_pages, (4 * ppb, 2 * ppb, ppb))
    grid_spec = pltpu.PrefetchScalarGridSpec(
        num_scalar_prefetch=1,
        grid=(b, n_pages // pps),
        in_specs=_page_specs(pps, (page, nh, dh)),
        out_specs=pl.BlockSpec((1, pps // ppb, nh, dh), lambda bi, p, pt: (bi, p, 0, 0)),
    )
    return pl.pallas_call(
        functools.partial(_ksum_kernel, pages_per_block=ppb),
        grid_spec=grid_spec,
        out_shape=jax.ShapeDtypeStruct((b, n_pages // ppb, nh, dh), F32),
        compiler_params=_params("parallel", "arbitrary"),
        name="cache_block_ksum",
    )(page_table, *([cache_k] * pps))


def _sample_select_kernel(q_ref, ks_ref, sel_ref):
    nh = q_ref.shape[1]
    nblk = ks_ref.shape[2]
    gate = jnp.concatenate(
        [_dot_nt(ks_ref[0, h], q_ref[0, h].astype(F32), _HI) for h in range(nh)], axis=1)
    row = lax.broadcasted_iota(jnp.int32, gate.shape, 0)
    cnt = jnp.zeros(gate.shape, jnp.int32)
    for n in range(nblk):
        g_n = gate[n:n + 1, :]
        cnt = cnt + jnp.where((g_n > gate) | ((g_n == gate) & (n < row)), 1, 0)
    sel_ref[0] = jnp.where(cnt < MOBA_TOPK, 0.0, NEG_INF)


def sample_select(q_hq, ksum_h):
    b, nh, tq, dh = q_hq.shape
    nblk = ksum_h.shape[2]
    return pl.pallas_call(
        _sample_select_kernel,
        grid=(b,),
        in_specs=[pl.BlockSpec((1, nh, tq, dh), lambda i: (i, 0, 0, 0)),
                  pl.BlockSpec((1, nh, nblk, dh), lambda i: (i, 0, 0, 0))],
        out_specs=pl.BlockSpec((1, nblk, nh * tq), lambda i: (i, 0, 0)),
        out_shape=jax.ShapeDtypeStruct((b, nblk, nh * tq), F32),
        compiler_params=_params("parallel"),
        name="sample_select",
    )(q_hq, ksum_h)


def _sample_bias_kernel(rt_ref, tiles_ref, new_ref, *, n_heads, tq, page):
    table = lambda b: rt_ref[:, b:b + 1]
    ri = lax.broadcasted_iota(jnp.int32, tiles_ref.shape[1:], 0)
    ci = lax.broadcasted_iota(jnp.int32, tiles_ref.shape[1:], 1)
    same = ci % n_heads == ri // tq
    tiles_ref[0] = jnp.where(same, table(REL_BUCKETS - 1), NEG_INF)
    tiles_ref[1] = jnp.where(same, _rel_bias_of_dist(page + ri % tq - ci // n_heads, table), NEG_INF)
    ri = lax.broadcasted_iota(jnp.int32, new_ref.shape, 0)
    ci = lax.broadcasted_iota(jnp.int32, new_ref.shape, 1)
    d = ri % tq - ci // n_heads
    ok = (d >= 0) & (ci % n_heads == ri // tq)
    new_ref[...] = jnp.where(ok, _rel_bias_of_dist(d, table), NEG_INF)


def sample_bias(rel_bias, tq, page):
    nh = rel_bias.shape[1]
    rows = nh * tq
    rt = jnp.repeat(rel_bias.T, tq, axis=0)
    return pl.pallas_call(
        functools.partial(_sample_bias_kernel, n_heads=nh, tq=tq, page=page),
        out_shape=[jax.ShapeDtypeStruct((2, rows, page * nh), F32), jax.ShapeDtypeStruct((rows, rows), F32)],
        compiler_params=pltpu.CompilerParams(vmem_limit_bytes=V7X_VMEM_LIMIT),
        name="sample_bias",
    )(rt)


def _moba_sample_kernel(pt_ref, q_ref, kn_ref, vn_ref, *rest, ppb, pps):
    k_refs, v_refs, tile_refs = rest[:pps], rest[pps:2 * pps], rest[2 * pps:3 * pps]
    sel_ref, new_ref, o_ref, m_ref, l_ref, acc_ref = rest[3 * pps:]
    p = pl.program_id(1)
    scale = ATTN_HEAD_DIM ** -0.5
    q = q_ref[0]
    rows = q.shape[0]

    @pl.when(p == 0)
    def _():
        lm = _dot_nt(q, kn_ref[0]) * scale + new_ref[...]
        m = jnp.max(lm, axis=-1, keepdims=True)
        pr = jnp.exp(lm - m)
        m_ref[...] = m
        l_ref[...] = jnp.sum(pr, axis=-1, keepdims=True)
        acc_ref[...] = _dot(pr.astype(BF16), vn_ref[0])

    ri = lax.broadcasted_iota(jnp.int32, (rows, rows), 0)
    ci = lax.broadcasted_iota(jnp.int32, (rows, rows), 1)
    picked = [jnp.sum(jnp.where(ri == ci, sel_ref[0, blk], 0.0), axis=-1, keepdims=True) for blk in range(pps // ppb)]
    lm = [_dot_nt(q, k_refs[i][0].astype(BF16)) * scale + tile_refs[i][0] for i in range(pps)]
    m_old = m_ref[...]
    m_new = m_old
    for i in range(pps):
        m_new = jnp.maximum(m_new, jnp.max(lm[i], axis=-1, keepdims=True) + picked[i // ppb])
    alpha = jnp.exp(m_old - m_new)
    shift = [m_new - pk for pk in picked]
    pr = [jnp.exp(lm[i] - shift[i // ppb]) for i in range(pps)]
    m_ref[...] = m_new
    l_ref[...] = alpha * l_ref[...] + sum(jnp.sum(x, axis=-1, keepdims=True) for x in pr)
    acc_ref[...] = alpha * acc_ref[...] + sum(
        _dot(pr[i].astype(BF16), v_refs[i][0].astype(BF16)) for i in range(pps))

    @pl.when(p == pl.num_programs(1) - 1)
    def _():
        o_ref[0] = acc_ref[...] / l_ref[...]


def moba_sample(q, k_new, v_new, cache_k, cache_v, page_table, rel_bias, tq):
    m, d = q.shape
    b = m // tq
    n_pages = page_table.shape[1]
    _, page, nh, dh = cache_k.shape
    ppb = MOBA_BLOCK // page
    rows = nh * tq
    past = n_pages * page
    assert MOBA_BLOCK % page == 0 and past % MOBA_BLOCK == 0 and tq <= MOBA_BLOCK and page + 1 >= T5_SATURATION

    q_hq = q.reshape(b, tq, nh, dh).transpose(0, 2, 1, 3)
    ksum = cache_block_ksum(cache_k, page_table).transpose(0, 2, 1, 3)
    nblk = past // MOBA_BLOCK
    sel = sample_select(q_hq, ksum).reshape(b, nblk, 1, rows)
    bias_tiles, bias_new = sample_bias(rel_bias, tq, page)
    q_rows = q_hq.reshape(b, rows, dh)
    kn = k_new.reshape(b, rows, dh)
    vn = v_new.reshape(b, rows, dh)

    cols = page * nh
    n_phys = cache_k.shape[0]
    ck = cache_k.reshape(n_phys, cols, dh)
    cv = cache_v.reshape(n_phys, cols, dh)
    small = pl.BlockSpec((1, rows, dh), lambda bi, p, pt: (bi, 0, 0))
    pps = _pick(n_pages, (2 * ppb, ppb))
    n_steps = n_pages // pps
    pages = _page_specs(pps, (cols, dh))
    far_tile = pl.BlockSpec((1, rows, cols), lambda bi, p, pt: (0, 0, 0))
    end_tile = pl.BlockSpec((1, rows, cols), lambda bi, p, pt: (jnp.where(p == n_steps - 1, 1, 0), 0, 0))
    grid_spec = pltpu.PrefetchScalarGridSpec(
        num_scalar_prefetch=1,
        grid=(b, n_steps),
        in_specs=[small, small, small] + pages + pages + [far_tile] * (pps - 1) + [end_tile]
        + [pl.BlockSpec((1, pps // ppb, 1, rows), lambda bi, p, pt: (bi, p, 0, 0)),
           pl.BlockSpec((rows, rows), lambda bi, p, pt: (0, 0))],
        out_specs=small,
        scratch_shapes=[pltpu.VMEM((rows, 1), F32), pltpu.VMEM((rows, 1), F32), pltpu.VMEM((rows, dh), F32)],
    )
    o = pl.pallas_call(
        functools.partial(_moba_sample_kernel, ppb=ppb, pps=pps),
        grid_spec=grid_spec,
        out_shape=jax.ShapeDtypeStruct((b, rows, dh), F32),
        compiler_params=_params("parallel", "arbitrary"),
        name="moba_sample",
    )(page_table, q_rows, kn, vn, *([ck] * pps), *([cv] * pps), *([bias_tiles] * pps), sel, bias_new)
    return o.reshape(b, nh, tq, dh).transpose(0, 2, 1, 3).reshape(m, d).astype(BF16)


def rwkv_time_mix(x, seq_len, shift0, wkv0, gain, w):
    m, d = x.shape
    (xr, xw, xk, xv, xa, xg), shift_new = rwkv_mix(x, gain, w["mu"], shift0, seq_len)
    r = matmul(xr, w["w_r"])
    k = matmul(xk, w["w_k"])
    v = matmul(xv, w["w_v"])
    wl = lora(xw, w["w1"], w["w2"], "tanh")
    al = lora(xa, w["a1"], w["a2"], "none")
    g = lora(xg, w["g1"], w["g2"], "sigmoid")
    y, h_fin = wkv(r, k, v, wl, al, g, w["prm"], jnp.swapaxes(wkv0, -1, -2), seq_len)
    x_new = matmul(y, w["w_o"], res=x)
    return x_new, jnp.swapaxes(h_fin, -1, -2), shift_new


def _rwkv_weights(i, rwkv_mu, rwkv_w_r, rwkv_w_k, rwkv_w_v, rwkv_w_o, rwkv_w0, rwkv_w1, rwkv_w2, rwkv_a0, rwkv_a1,
                  rwkv_a2, rwkv_g1, rwkv_g2, rwkv_k_k, rwkv_k_a, rwkv_r_k, rwkv_gn_g, rwkv_gn_b):
    d = rwkv_mu.shape[-1]
    w1, w2 = _pad_rank(rwkv_w1[i], rwkv_w2[i])
    a1, a2 = _pad_rank(rwkv_a1[i], rwkv_a2[i])
    g1, g2 = _pad_rank(rwkv_g1[i], rwkv_g2[i])
    prm = jnp.stack([rwkv_w0[i], rwkv_a0[i], rwkv_k_k[i], rwkv_k_a[i], rwkv_r_k[i].reshape(d), rwkv_gn_g[i],
                     rwkv_gn_b[i], jnp.zeros((d,), F32)]).astype(F32)
    return dict(mu=rwkv_mu[i], w_r=rwkv_w_r[i].astype(BF16), w_k=rwkv_w_k[i].astype(BF16),
                w_v=rwkv_w_v[i].astype(BF16), w_o=rwkv_w_o[i].astype(BF16),
                w1=w1, w2=w2, a1=a1, a2=a2, g1=g1, g2=g2, prm=prm)


def kernel(x_prompt, x_sample, state_wkv, state_shift, cache_k, cache_v, page_table, norm_mix, norm_ffn, norm_kv, norm_final, rwkv_mu, rwkv_w_r, rwkv_w_k, rwkv_w_v, rwkv_w_o, rwkv_w0, rwkv_w1, rwkv_w2, rwkv_a0, rwkv_a1, rwkv_a2, rwkv_g1, rwkv_g2, rwkv_k_k, rwkv_k_a, rwkv_r_k, rwkv_gn_g, rwkv_gn_b, kv_w, attn_w_q, attn_w_o, rel_bias, ffn_w_gate, ffn_w_up, ffn_w_down, moe_w_router, moe_b_router, moe_w_gate, moe_w_up, moe_w_down):
    d = x_prompt.shape[-1]
    nh = d // ATTN_HEAD_DIM
    assert norm_mix.shape[0] == 2 and rwkv_mu.shape[0] == 1 and attn_w_q.shape[0] == 1
    assert ffn_w_gate.shape[0] == 1 and moe_w_gate.shape[0] == 1

    w_rwkv = _rwkv_weights(0, rwkv_mu, rwkv_w_r, rwkv_w_k, rwkv_w_v, rwkv_w_o, rwkv_w0, rwkv_w1, rwkv_w2, rwkv_a0,
                           rwkv_a1, rwkv_a2, rwkv_g1, rwkv_g2, rwkv_k_k, rwkv_k_a, rwkv_r_k, rwkv_gn_g, rwkv_gn_b)
    wg, wu, wd = ffn_w_gate[0].astype(BF16), ffn_w_up[0].astype(BF16), ffn_w_down[0].astype(BF16)
    kvw_k, kvw_v = kv_w[:, :d].astype(BF16), kv_w[:, d:].astype(BF16)
    wq, wo = attn_w_q[0].astype(BF16), attn_w_o[0].astype(BF16)
    eg, eu, ed = moe_w_gate[0].astype(BF16), moe_w_up[0].astype(BF16), moe_w_down[0].astype(BF16)
    gains_kvq = jnp.stack([norm_kv, norm_mix[1]])

    def run_group(x3, wkv0, shift0, attend):
        b, t, _ = x3.shape
        x = x3.reshape(b * t, d)
        x, wkv_new, shift_new = rwkv_time_mix(x, t, shift0, wkv0, norm_mix[0], w_rwkv)
        x = ffn(x, norm_ffn[0], wg, wu, wd)
        h_kv, h_q = rmsnorm_multi(x, gains_kvq, [BF16, BF16])
        k, k_bf = matmul(h_kv, kvw_k, out_dtypes=(F32, BF16))
        v, v_bf = matmul(h_kv, kvw_v, out_dtypes=(F32, BF16))
        q = matmul(h_q, wq, out_dtypes=(BF16,))
        o = attend(q, k_bf, v_bf, t)
        x = matmul(o, wo, res=x)
        y = moe_layer(x, norm_ffn[1], norm_final, moe_w_router[0], moe_b_router[0], eg, eu, ed)
        kv_shape = (b, t, nh, ATTN_HEAD_DIM)
        return y.reshape(b, t, d), wkv_new[None], shift_new[None], k.reshape(kv_shape), v.reshape(kv_shape)

    bp = x_prompt.shape[0]
    wkv_zero = jnp.zeros((bp,) + state_wkv.shape[2:], state_wkv.dtype)
    shift_zero = jnp.zeros((bp, d), state_shift.dtype)
    out_p = run_group(
        x_prompt, wkv_zero, shift_zero,
        lambda q, k, v, t: moba_prompt(q, k, v, rel_bias, t))
    out_s = run_group(
        x_sample, state_wkv[0], state_shift[0],
        lambda q, k, v, t: moba_sample(q, k, v, cache_k, cache_v, page_table, rel_bias, t))
    return (out_p[0], out_s[0]) + out_p[1:] + out_s[1:]
```
